```python
import math
import jax
import jax.numpy as jnp
from jax import lax
import numpy as np

D_MODEL = 1024
BATCH = 16
SEQ = 2048
DEPTH = 4
DEC_BATCH = 8
DEC_SEQ = 16
PAST_LEN = 2048

CHUNK = 64
QBLOCK = 128
N_MIXERS = 4
ALPHA = (2 * DEPTH) ** 0.25
BETA = (8 * DEPTH) ** -0.25
LN_EPS = 1e-5
RMS_EPS = 1e-6
NEG_INF = -1e30

DA_HEADS = 8
DA_HEAD_DIM = D_MODEL // DA_HEADS // 2
DA_QK = DA_HEADS * 2 * DA_HEAD_DIM
DA_V = DA_HEADS * 2 * DA_HEAD_DIM

MLA_HEADS = 16
MLA_NOPE_DIM = 64
MLA_ROPE_DIM = 32
MLA_QK_DIM = MLA_NOPE_DIM + MLA_ROPE_DIM
MLA_V_DIM = 64
MLA_Q_RANK = 384
MLA_KV_RANK = 256
ROPE_THETA = 10000.0

POOL_WINDOWS = (2, 4, 8, 16)
POOL_GROUP = D_MODEL // len(POOL_WINDOWS)
POOL_STATE = max(POOL_WINDOWS) - 1

GLA_HEADS = 4
GLA_DK = D_MODEL // 2 // GLA_HEADS
GLA_DV = D_MODEL // GLA_HEADS
GLA_KEY = GLA_HEADS * GLA_DK
GLA_VAL = GLA_HEADS * GLA_DV
GLA_GATE_RANK = 16
GLA_TAU = 16.0
GLA_BLOCK = CHUNK

D_FF = 2816
N_EXPERTS = 8
TOP_K = 2
MOE_D_FF = 1408
N_DENSE = (DEPTH + 1) // 2
N_MOE = DEPTH // 2

kernel_name = 'hybrid_chunk_stream_encoder_step'


def rms_norm(x, g):
    xf = x.astype(jnp.float32)
    y = xf * lax.rsqrt(jnp.mean(xf * xf, axis=-1, keepdims=True) + RMS_EPS)
    return (y * g.astype(jnp.float32)).astype(x.dtype)


def layer_norm(x, g, b):
    xf = x.astype(jnp.float32)
    mu = jnp.mean(xf, axis=-1, keepdims=True)
    xc = xf - mu
    var = jnp.mean(xc * xc, axis=-1, keepdims=True)
    y = xc * lax.rsqrt(var + LN_EPS) * g.astype(jnp.float32) + b.astype(jnp.float32)
    return y.astype(x.dtype)


def apply_rope(x, pos):
    half = x.shape[-1] // 2
    inv_freq = ROPE_THETA ** (-jnp.arange(half, dtype=jnp.float32) / half)
    ang = pos.astype(jnp.float32)[:, None] * inv_freq[None, :]
    ang = ang.reshape((pos.shape[0],) + (1,) * (x.ndim - 3) + (half,))
    cos, sin = jnp.cos(ang), jnp.sin(ang)
    xf = x.astype(jnp.float32)
    x1, x2 = xf[..., :half], xf[..., half:]
    return jnp.concatenate([x1 * cos - x2 * sin, x2 * cos + x1 * sin], axis=-1).astype(x.dtype)


def chunk_causal_attention(q, k, v, q_pos, k_pos, scale):
    b, tq = q.shape[0], q.shape[1]
    qb = QBLOCK if tq % QBLOCK == 0 else tq
    nb = tq // qb
    k_chunk = k_pos // CHUNK

    def one_block(blk):
        q_blk, p_blk = blk
        s = jnp.einsum('bqhgd,bkhgd->bhgqk', q_blk, k).astype(jnp.float32) * scale
        visible = k_chunk[None, :] <= (p_blk // CHUNK)[:, None]
        s = jnp.where(visible, s, NEG_INF)
        p = jax.nn.softmax(s, axis=-1).astype(v.dtype)
        return jnp.einsum('bhgqk,bkhe->bqhge', p, v)

    q_blocks = jnp.swapaxes(q.reshape((b, nb, qb) + q.shape[2:]), 0, 1)
    out = lax.map(one_block, (q_blocks, q_pos.reshape(nb, qb)))
    out = jnp.swapaxes(out, 0, 1)
    return out.reshape((b, tq) + out.shape[3:])


def diff_attention(x, past_k, past_v, lambda_init, w_qkv, lam_q1, lam_k1, lam_q2, lam_k2, subln_g, w_o):
    b, t, _ = x.shape
    p = past_k.shape[1]
    q, k, v = jnp.split(x @ w_qkv, [DA_QK, 2 * DA_QK], axis=-1)
    q = q.reshape(b, t, DA_HEADS, 2, DA_HEAD_DIM)
    k = k.reshape(b, t, DA_HEADS, 2, DA_HEAD_DIM)
    v = v.reshape(b, t, DA_HEADS, 2 * DA_HEAD_DIM)
    k_all = jnp.concatenate([past_k, k], axis=1)
    v_all = jnp.concatenate([past_v, v], axis=1)
    o = chunk_causal_attention(q, k_all, v_all, p + jnp.arange(t), jnp.arange(p + t),
                               DA_HEAD_DIM ** -0.5)
    lam = (jnp.exp(jnp.sum(lam_q1.astype(jnp.float32) * lam_k1.astype(jnp.float32)))
           - jnp.exp(jnp.sum(lam_q2.astype(jnp.float32) * lam_k2.astype(jnp.float32)))
           + lambda_init)
    o = o[:, :, :, 0] - lam.astype(o.dtype) * o[:, :, :, 1]
    o = rms_norm(o, subln_g) * (1.0 - lambda_init)
    return o.reshape(b, t, DA_V) @ w_o, k, v


def mla_attention(x, past_ckv, past_kr, w_dq, q_norm_g, w_uq, w_dkv, kv_norm_g, w_uk, w_uv, w_o):
    b, t, _ = x.shape
    p = past_ckv.shape[1]
    tk = p + t
    pos = p + jnp.arange(t)
    q = (rms_norm(x @ w_dq, q_norm_g) @ w_uq).reshape(b, t, MLA_HEADS, MLA_QK_DIM)
    q = jnp.concatenate([q[..., :MLA_NOPE_DIM], apply_rope(q[..., MLA_NOPE_DIM:], pos)], axis=-1)
    kv_a = x @ w_dkv
    ckv = rms_norm(kv_a[..., :MLA_KV_RANK], kv_norm_g)
    kr = apply_rope(kv_a[..., MLA_KV_RANK:], pos)
    ckv_all = jnp.concatenate([past_ckv, ckv], axis=1)
    kr_all = jnp.concatenate([past_kr, kr], axis=1)
    k_nope = (ckv_all @ w_uk).reshape(b, tk, MLA_HEADS, MLA_NOPE_DIM)
    v = (ckv_all @ w_uv).reshape(b, tk, MLA_HEADS, MLA_V_DIM)
    k = jnp.concatenate(
        [k_nope, jnp.broadcast_to(kr_all[:, :, None, :], (b, tk, MLA_HEADS, MLA_ROPE_DIM))], axis=-1)
    o = chunk_causal_attention(q[:, :, :, None], k[:, :, :, None], v, pos, jnp.arange(tk),
                               MLA_QK_DIM ** -0.5)
    return o.reshape(b, t, MLA_HEADS * MLA_V_DIM) @ w_o, ckv, kr


def pool_mixer(x, prev_rows, pos0, w_pool, scale):
    b, t, d = x.shape
    ext = jnp.concatenate([prev_rows, x], axis=1).astype(jnp.float32)
    csum = jnp.concatenate([jnp.zeros((b, 1, d), jnp.float32), jnp.cumsum(ext, axis=1)], axis=1)
    pos = pos0 + jnp.arange(t)
    end = POOL_STATE + 1
    groups = []
    for gi, w in enumerate(POOL_WINDOWS):
        ch = slice(gi * POOL_GROUP, (gi + 1) * POOL_GROUP)
        win_sum = csum[:, end:end + t, ch] - csum[:, end - w:end - w + t, ch]
        count = jnp.minimum(pos + 1, w).astype(jnp.float32)
        groups.append(win_sum / count[None, :, None])
    mixed = jnp.concatenate(groups, axis=-1) - ext[:, POOL_STATE:]
    mixed = mixed.reshape(b, t, len(POOL_WINDOWS), POOL_GROUP).astype(x.dtype)
    y = jnp.einsum('btgc,gce->btge', mixed, w_pool).reshape(b, t, d) * scale
    return y, ext[:, -POOL_STATE:].astype(x.dtype)


def gla_mixer(x, s0, w_in, w_a1, w_a2, b_a, head_g, w_o):
    f32 = jnp.float32
    b, t, _ = x.shape
    blk = GLA_BLOCK if t % GLA_BLOCK == 0 else t
    n = t // blk
    q, k, v, gate = jnp.split(x @ w_in, [GLA_KEY, 2 * GLA_KEY, 2 * GLA_KEY + GLA_VAL], axis=-1)
    log_a = jax.nn.log_sigmoid(((x @ w_a1) @ w_a2 + b_a).astype(f32)) / GLA_TAU
    shp_k = (b, n, blk, GLA_HEADS, GLA_DK)
    q = q.astype(f32).reshape(shp_k) * GLA_DK ** -0.5
    k = k.astype(f32).reshape(shp_k)
    v = v.astype(f32).reshape(b, n, blk, GLA_HEADS, GLA_DV)
    cum = jnp.cumsum(log_a.reshape(shp_k), axis=2)
    cum_last = cum[:, :, -1]
    q_dec = q * jnp.exp(cum)
    k_inv = k * jnp.exp(-cum)
    k_end = k * jnp.exp(cum_last[:, :, None] - cum)
    causal = jnp.tril(jnp.ones((blk, blk), dtype=bool))
    scores = jnp.where(causal, jnp.einsum('bnthd,bnshd->bnhts', q_dec, k_inv), 0.0)
    o_intra = jnp.einsum('bnhts,bnshv->bnthv', scores, v)
    d_state = jnp.einsum('bnshd,bnshv->bnhdv', k_end, v)

    def step(state, inp):
        decay, ds = inp
        return decay[..., None] * state + ds, state

    s_final, s_before = lax.scan(step, s0.astype(f32),
                                 (jnp.swapaxes(jnp.exp(cum_last), 0, 1), jnp.swapaxes(d_state, 0, 1)))
    o = o_intra + jnp.einsum('bnthd,bnhdv->bnthv', q_dec, jnp.swapaxes(s_before, 0, 1))
    o = rms_norm(o.reshape(b, t, GLA_HEADS, GLA_DV), head_g) * \
        jax.nn.silu(gate.astype(f32)).reshape(b, t, GLA_HEADS, GLA_DV)
    return o.reshape(b, t, GLA_VAL).astype(x.dtype) @ w_o, s_final.astype(s0.dtype)


def swiglu(x, w_in, w_out):
    gate, up = jnp.split(x @ w_in, 2, axis=-1)
    return (jax.nn.silu(gate) * up) @ w_out


def moe_swiglu(x, w_router, b_router, w_in, w_out):
    logits = jnp.einsum('btd,de->bte', x, w_router).astype(jnp.float32) + b_router.astype(jnp.float32)
    top_val, top_idx = lax.top_k(logits, TOP_K)
    top_w = jax.nn.softmax(top_val, axis=-1)
    combine = jnp.sum(jax.nn.one_hot(top_idx, N_EXPERTS, dtype=jnp.float32) * top_w[..., None], axis=-2)
    y = jnp.zeros_like(x)
    for e in range(N_EXPERTS):
        y = y + combine[..., e:e + 1].astype(x.dtype) * swiglu(x, w_in[e], w_out[e])
    return y


def setup_inputs(seed: int = 0) -> dict:
    key = jax.random.key(seed)
    n_keys = 48
    keys = jax.random.split(key, n_keys)
    ks = iter([keys[i] for i in range(n_keys)])

    def nrm(shape, scale=1.0):
        return jax.random.normal(next(ks), shape, jnp.float32) * scale

    def gain(shape):
        return 1.0 + nrm(shape, 0.02)

    d = D_MODEL
    return {
        'x_prompt': nrm((BATCH, SEQ, d)),
        'x_sample': nrm((DEC_BATCH, DEC_SEQ, d)),
        'cache_a_k': nrm((DEC_BATCH, PAST_LEN, DA_HEADS, 2, DA_HEAD_DIM)),
        'cache_a_v': nrm((DEC_BATCH, PAST_LEN, DA_HEADS, 2 * DA_HEAD_DIM)),
        'cache_b_ckv': nrm((DEC_BATCH, PAST_LEN, MLA_KV_RANK)),
        'cache_b_krope': nrm((DEC_BATCH, PAST_LEN, MLA_ROPE_DIM)),
        'state_c_rows': nrm((DEC_BATCH, POOL_STATE, d)),
        'state_d_gla': nrm((DEC_BATCH, GLA_HEADS, GLA_DK, GLA_DV)),
        'a_w_qkv': nrm((d, 2 * DA_QK + DA_V), d ** -0.5),
        'a_lam_q1': nrm((DA_HEAD_DIM,), 0.1),
        'a_lam_k1': nrm((DA_HEAD_DIM,), 0.1),
        'a_lam_q2': nrm((DA_HEAD_DIM,), 0.1),
        'a_lam_k2': nrm((DA_HEAD_DIM,), 0.1),
        'a_subln_g': gain((2 * DA_HEAD_DIM,)),
        'a_w_o': nrm((DA_V, d), DA_V ** -0.5 * BETA),
        'b_w_dq': nrm((d, MLA_Q_RANK), d ** -0.5),
        'b_q_norm_g': gain((MLA_Q_RANK,)),
        'b_w_uq': nrm((MLA_Q_RANK, MLA_HEADS * MLA_QK_DIM), MLA_Q_RANK ** -0.5),
        'b_w_dkv': nrm((d, MLA_KV_RANK + MLA_ROPE_DIM), d ** -0.5),
        'b_kv_norm_g': gain((MLA_KV_RANK,)),
        'b_w_uk': nrm((MLA_KV_RANK, MLA_HEADS * MLA_NOPE_DIM), MLA_KV_RANK ** -0.5),
        'b_w_uv': nrm((MLA_KV_RANK, MLA_HEADS * MLA_V_DIM), MLA_KV_RANK ** -0.5),
        'b_w_o': nrm((MLA_HEADS * MLA_V_DIM, d), (MLA_HEADS * MLA_V_DIM) ** -0.5 * BETA),
        'c_w_pool': nrm((len(POOL_WINDOWS), POOL_GROUP, POOL_GROUP), POOL_GROUP ** -0.5 * BETA),
        'c_scale': gain((d,)),
        'd_w_in': nrm((d, 2 * GLA_KEY + 2 * GLA_VAL), d ** -0.5),
        'd_w_a1': nrm((d, GLA_GATE_RANK), d ** -0.5),
        'd_w_a2': nrm((GLA_GATE_RANK, GLA_KEY), GLA_GATE_RANK ** -0.5),
        'd_b_a': nrm((GLA_KEY,), 0.1),
        'd_head_g': gain((GLA_DV,)),
        'd_w_o': nrm((GLA_VAL, d), GLA_VAL ** -0.5 * BETA),
        'ln_g': gain((DEPTH, 2, d)),
        'ln_b': nrm((DEPTH, 2, d), 0.02),
        'ffn_w_in': nrm((N_DENSE, d, 2 * D_FF), d ** -0.5),
        'ffn_w_out': nrm((N_DENSE, D_FF, d), D_FF ** -0.5 * BETA),
        'moe_w_router': nrm((N_MOE, d, N_EXPERTS), d ** -0.5),
        'moe_b_router': nrm((N_MOE, N_EXPERTS), 0.01),
        'moe_w_in': nrm((N_MOE, N_EXPERTS, d, 2 * MOE_D_FF), d ** -0.5),
        'moe_w_out': nrm((N_MOE, N_EXPERTS, MOE_D_FF, d), MOE_D_FF ** -0.5 * BETA),
    }


def reference(x_prompt, x_sample, cache_a_k, cache_a_v, cache_b_ckv, cache_b_krope, state_c_rows,
              state_d_gla, a_w_qkv, a_lam_q1, a_lam_k1, a_lam_q2, a_lam_k2, a_subln_g, a_w_o,
              b_w_dq, b_q_norm_g, b_w_uq, b_w_dkv, b_kv_norm_g, b_w_uk, b_w_uv, b_w_o,
              c_w_pool, c_scale, d_w_in, d_w_a1, d_w_a2, d_b_a, d_head_g, d_w_o,
              ln_g, ln_b, ffn_w_in, ffn_w_out, moe_w_router, moe_b_router, moe_w_in, moe_w_out):
    dt = x_prompt.dtype
    bp = x_prompt.shape[0]
    past_len = cache_a_k.shape[1]
    xp, xs = x_prompt, x_sample
    for layer in range(DEPTH):
        mixer = layer % N_MIXERS
        if mixer == 0:
            lambda_init = 0.8 - 0.6 * math.exp(-0.3 * layer)
            a_params = (a_w_qkv, a_lam_q1, a_lam_k1, a_lam_q2, a_lam_k2, a_subln_g, a_w_o)
            yp, a_k_p, a_v_p = diff_attention(
                xp, jnp.zeros((bp, 0, DA_HEADS, 2, DA_HEAD_DIM), dt),
                jnp.zeros((bp, 0, DA_HEADS, 2 * DA_HEAD_DIM), dt), lambda_init, *a_params)
            ys, a_k_s, a_v_s = diff_attention(xs, cache_a_k, cache_a_v, lambda_init, *a_params)
        elif mixer == 1:
            b_params = (b_w_dq, b_q_norm_g, b_w_uq, b_w_dkv, b_kv_norm_g, b_w_uk, b_w_uv, b_w_o)
            yp, ckv_p, kr_p = mla_attention(
                xp, jnp.zeros((bp, 0, MLA_KV_RANK), dt), jnp.zeros((bp, 0, MLA_ROPE_DIM), dt), *b_params)
            ys, ckv_s, kr_s = mla_attention(xs, cache_b_ckv, cache_b_krope, *b_params)
        elif mixer == 2:
            yp, rows_p = pool_mixer(xp, jnp.zeros((bp, POOL_STATE, D_MODEL), dt), 0, c_w_pool, c_scale)
            ys, rows_s = pool_mixer(xs, state_c_rows, past_len, c_w_pool, c_scale)
        else:
            d_params = (d_w_in, d_w_a1, d_w_a2, d_b_a, d_head_g, d_w_o)
            yp, gla_p = gla_mixer(
                xp, jnp.zeros((bp, GLA_HEADS, GLA_DK, GLA_DV), state_d_gla.dtype), *d_params)
            ys, gla_s = gla_mixer(xs, state_d_gla, *d_params)
        xp = layer_norm(ALPHA * xp + yp, ln_g[layer, 0], ln_b[layer, 0])
        xs = layer_norm(ALPHA * xs + ys, ln_g[layer, 0], ln_b[layer, 0])
        j = layer // 2
        if layer % 2 == 0:
            fp = swiglu(xp, ffn_w_in[j], ffn_w_out[j])
            fs = swiglu(xs, ffn_w_in[j], ffn_w_out[j])
        else:
            fp = moe_swiglu(xp, moe_w_router[j], moe_b_router[j], moe_w_in[j], moe_w_out[j])
            fs = moe_swiglu(xs, moe_w_router[j], moe_b_router[j], moe_w_in[j], moe_w_out[j])
        xp = layer_norm(ALPHA * xp + fp, ln_g[layer, 1], ln_b[layer, 1])
        xs = layer_norm(ALPHA * xs + fs, ln_g[layer, 1], ln_b[layer, 1])
    return (xp, xs, a_k_p, a_v_p, a_k_s, a_v_s, ckv_p, kr_p, ckv_s, kr_s, rows_p, rows_s, gla_p, gla_s)
```

```python
import functools
import math

import jax
import jax.numpy as jnp
from jax import lax
from jax.experimental import pallas as pl
from jax.experimental.pallas import tpu as pltpu

F32 = jnp.float32
BF16 = jnp.bfloat16
I32 = jnp.int32

D_MODEL = 1024
DEPTH = 4
CHUNK = 64
ALPHA = (2 * DEPTH) ** 0.25
LN_EPS = 1e-5
RMS_EPS = 1e-6
NEG_INF = -1e30

DA_HEADS = 8
DA_HEAD_DIM = 64
MLA_HEADS = 16
MLA_NOPE_DIM = 64
MLA_ROPE_DIM = 32
MLA_QK_DIM = MLA_NOPE_DIM + MLA_ROPE_DIM
MLA_V_DIM = 64
MLA_Q_RANK = 384
MLA_KV_RANK = 256
ROPE_THETA = 10000.0
POOL_WINDOWS = (2, 4, 8, 16)
POOL_GROUP = D_MODEL // len(POOL_WINDOWS)
POOL_STATE = max(POOL_WINDOWS) - 1
GLA_HEADS = 4
GLA_DK = 128
GLA_DV = 256
GLA_KEY = GLA_HEADS * GLA_DK
GLA_VAL = GLA_HEADS * GLA_DV
GLA_GATE_RANK = 16
GLA_TAU = 16.0
GLA_BLOCK = CHUNK
N_EXPERTS = 8

LANES = 128
MXU_DIM = 256
VMEM_LIMIT_BYTES = 56 * 1024 * 1024
ROW_TILE = 512
HALO = 16


def _cparams(n_axes):
    return pltpu.CompilerParams(
        dimension_semantics=("arbitrary",) * n_axes, vmem_limit_bytes=VMEM_LIMIT_BYTES)


def _row_tile(m):
    return ROW_TILE if m % ROW_TILE == 0 else m


def _resident(shape):
    nd = len(shape)
    return pl.BlockSpec(shape, lambda *_: (0,) * nd)


def _layer_norm(z, g, b):
    mu = jnp.mean(z, axis=-1, keepdims=True)
    zc = z - mu
    var = jnp.mean(zc * zc, axis=-1, keepdims=True)
    return zc * lax.rsqrt(var + LN_EPS) * g + b


def _rms(z, g):
    return z * lax.rsqrt(jnp.mean(z * z, axis=-1, keepdims=True) + RMS_EPS) * g


def _dot(a, b):
    return jnp.dot(a, b, preferred_element_type=F32)


def _dot_nt(a, b):
    return lax.dot_general(a, b, (((1,), (1,)), ((), ())), preferred_element_type=F32)


def _dot_tn(a, b):
    return lax.dot_general(a, b, (((0,), (0,)), ((), ())), preferred_element_type=F32)


def _proj_body(x_ref, w_ref, *o_refs, groups, chunk):
    xb = x_ref[...].astype(BF16)
    o_refs = iter(o_refs)
    for c0, c1, sinks in groups:
        refs = [next(o_refs) for _ in sinks]
        for j0 in range(c0, c1, chunk):
            j1 = min(j0 + chunk, c1)
            r = _dot(xb, w_ref[:, j0:j1])
            for o_ref, (_, sc) in zip(refs, sinks):
                o_ref[:, j0 - c0:j1 - c0] = (r if sc == 1.0 else r * sc).astype(o_ref.dtype)


def _proj(x, w, groups, name):
    m, k = x.shape
    tm = _row_tile(m)
    outs = [(c1 - c0, dt) for c0, c1, sinks in groups for dt, _ in sinks]
    return pl.pallas_call(
        functools.partial(_proj_body, groups=groups, chunk=2 * MXU_DIM),
        grid=(m // tm,),
        in_specs=[pl.BlockSpec((tm, k), lambda i: (i, 0)), _resident(w.shape)],
        out_specs=[pl.BlockSpec((tm, wd), lambda i: (i, 0)) for wd, _ in outs],
        out_shape=[jax.ShapeDtypeStruct((m, wd), dt) for wd, dt in outs],
        compiler_params=_cparams(1),
        name=name,
    )(x, w)


def _oproj_ln_body(a_ref, w_ref, x_ref, g_ref, b_ref, o_ref):
    y = _dot(a_ref[...], w_ref[...])
    o_ref[...] = _layer_norm(ALPHA * x_ref[...] + y, g_ref[...], b_ref[...])


def _oproj_ln(a, w, x, g, b, name):
    m, kd = a.shape
    tm = _row_tile(m)
    row = lambda i: (i, 0)
    return pl.pallas_call(
        _oproj_ln_body,
        grid=(m // tm,),
        in_specs=[pl.BlockSpec((tm, kd), row), _resident(w.shape), pl.BlockSpec((tm, D_MODEL), row),
                  _resident((1, D_MODEL)), _resident((1, D_MODEL))],
        out_specs=pl.BlockSpec((tm, D_MODEL), row),
        out_shape=jax.ShapeDtypeStruct((m, D_MODEL), F32),
        compiler_params=_cparams(1),
        name=name,
    )(a, w, x, g.reshape(1, D_MODEL), b.reshape(1, D_MODEL))


def _attn_plan(tq_total, tk_total):
    past = tk_total - tq_total
    if past == 0:
        t = min(ROW_TILE, tq_total)
        assert tq_total % t == 0 and t % CHUNK == 0
        return t, t, tq_total // t, None, t
    tq = tq_total
    first_chunk = past // CHUNK
    all_visible = min((first_chunk + 1) * CHUNK, tk_total)
    tk = ROW_TILE
    n_full = all_visible // tk
    last_chunk = (past + tq - 1) // CHUNK
    end = min((last_chunk + 1) * CHUNK, tk_total)
    return tq, tk, 1, n_full, end - n_full * tk


def _attn_body(*refs, mode, tq, tk, n_full_static, tail, past, scale, lambda_init):
    if mode == "diff":
        lam_ref, q_ref, k_ref, v_ref, g_ref, o_ref, m_sc, l_sc, acc_sc = refs
    else:
        q_ref, k_ref, v_ref, o_ref, m_sc, l_sc, acc_sc = refs
    qi = pl.program_id(2)
    n_full = qi if n_full_static is None else n_full_static
    lane = lax.broadcasted_iota(I32, (1, LANES), 1)
    lo = lane < LANES // 2

    q = q_ref[0]
    if mode == "diff":
        q0 = jnp.where(lo, q, jnp.zeros_like(q))
        q1 = jnp.where(lo, jnp.zeros_like(q), q)
    else:
        q0 = q[:, :LANES]
        q1 = q[:, LANES:]

    m_sc[...] = jnp.full(m_sc.shape, NEG_INF, F32)
    l_sc[...] = jnp.zeros(l_sc.shape, F32)
    acc_sc[...] = jnp.zeros(acc_sc.shape, F32)

    def tile(off, length, masked):
        k = k_ref[0, pl.ds(off, length), :]
        v = v_ref[0, pl.ds(off, length), :]
        if mode == "diff":
            s = jnp.concatenate([_dot_nt(q0, k), _dot_nt(q1, k)], axis=0)
        else:
            s = jnp.concatenate([_dot_nt(q0, k[:, :LANES]), _dot_nt(q1, k[:, LANES:])], axis=0)
            s = s * scale
        if masked:
            row = lax.broadcasted_iota(I32, (2 * tq, length), 0)
            row = jnp.where(row >= tq, row - tq, row)
            q_chunk = (past + qi * tq + row) // CHUNK
            k_chunk = (off + lax.broadcasted_iota(I32, (2 * tq, length), 1)) // CHUNK
            s = jnp.where(k_chunk <= q_chunk, s, NEG_INF)
        m_old = m_sc[...]
        m_new = jnp.maximum(m_old, jnp.max(s, axis=-1, keepdims=True))
        alpha = jnp.exp(m_old - m_new)
        p = jnp.exp(s - m_new)
        l_sc[...] = alpha * l_sc[...] + jnp.sum(p, axis=-1, keepdims=True)
        m_sc[...] = m_new
        pb = p.astype(BF16)
        if mode == "diff":
            acc_sc[...] = alpha * acc_sc[...] + _dot(pb, v)
        else:
            v0 = jnp.where(lo, v, jnp.zeros_like(v))
            v1 = jnp.where(lo, jnp.zeros_like(v), v)
            alpha_l = jnp.where(lo, alpha[:tq], alpha[tq:])
            acc_sc[...] = alpha_l * acc_sc[...] + _dot(pb[:tq], v0) + _dot(pb[tq:], v1)

    def full_tile(j, carry):
        tile(pl.multiple_of(j * tk, tk), tk, False)
        return carry

    lax.fori_loop(0, n_full, full_tile, 0)
    tile(n_full * tk if n_full_static is not None else pl.multiple_of(n_full * tk, tk), tail, True)

    inv_l = 1.0 / l_sc[...]
    if mode == "diff":
        lam_v = lam_ref[...]
        lam = (jnp.exp(jnp.sum(lam_v[0:1] * lam_v[1:2], keepdims=True))
               - jnp.exp(jnp.sum(lam_v[2:3] * lam_v[3:4], keepdims=True)) + lambda_init)
        acc = acc_sc[...]
        o = acc[:tq] * inv_l[:tq] - lam * (acc[tq:] * inv_l[tq:])
        o = _rms(o, g_ref[...]) * (1.0 - lambda_init)
    else:
        o = acc_sc[...] * jnp.where(lo, inv_l[:tq], inv_l[tq:])
    o_ref[0] = o.astype(o_ref.dtype)


def _attention(mode, q, k, v, *, lam_vecs=None, subln_g=None, lambda_init=0.0, scale=1.0, name):
    bsz, tq_total, _ = q.shape
    tk_total = k.shape[1]
    qw = LANES if mode == "diff" else 2 * LANES
    heads = q.shape[2] // qw
    tq, tk, nq, n_full, tail = _attn_plan(tq_total, tk_total)
    body = functools.partial(
        _attn_body, mode=mode, tq=tq, tk=tk, n_full_static=n_full, tail=tail,
        past=tk_total - tq_total, scale=scale, lambda_init=lambda_init)
    in_specs = [
        pl.BlockSpec((1, tq, qw), lambda b, h, i: (b, i, h)),
        pl.BlockSpec((1, tk_total, qw), lambda b, h, i: (b, 0, h)),
        pl.BlockSpec((1, tk_total, LANES), lambda b, h, i: (b, 0, h)),
    ]
    args = [q, k, v]
    if mode == "diff":
        in_specs = [_resident((4, DA_HEAD_DIM))] + in_specs + [_resident((1, LANES))]
        args = [lam_vecs] + args + [subln_g.reshape(1, LANES)]
    return pl.pallas_call(
        body,
        grid=(bsz, heads, nq),
        in_specs=in_specs,
        out_specs=pl.BlockSpec((1, tq, LANES), lambda b, h, i: (b, i, h)),
        out_shape=jax.ShapeDtypeStruct((bsz, tq_total, heads * LANES), BF16),
        scratch_shapes=[pltpu.VMEM((2 * tq, 1), F32), pltpu.VMEM((2 * tq, 1), F32),
                        pltpu.VMEM((2 * tq if mode == "diff" else tq, LANES), F32)],
        compiler_params=_cparams(3),
        name=name,
    )(*args)


def _ffn_body(te_ref, na_ref, x_ref, win_ref, wout_ref, *rest, dff, fused_ln):
    if fused_ln:
        g_ref, b_ref, o_ref = rest
    else:
        (o_ref,) = rest
    i = pl.program_id(0)

    @pl.when(i < na_ref[0])
    def _():
        x = x_ref[...]
        xb = x.astype(BF16)
        acc = jnp.zeros((x.shape[0], D_MODEL), F32)
        for c0 in range(0, dff, MXU_DIM):
            c1 = min(c0 + MXU_DIM, dff)
            gate = _dot(xb, win_ref[0, :, c0:c1])
            up = _dot(xb, win_ref[0, :, dff + c0:dff + c1])
            act = (jax.nn.silu(gate) * up).astype(BF16)
            acc = acc + _dot(act, wout_ref[0, c0:c1, :])
        if fused_ln:
            acc = _layer_norm(ALPHA * x + acc, g_ref[...], b_ref[...])
        o_ref[...] = acc

    @pl.when(i >= na_ref[0])
    def _():
        o_ref[...] = jnp.zeros(o_ref.shape, F32)


def _ffn(x, w_in, w_out, tile_expert, n_active, tm, ln=None, name="ffn"):
    m = x.shape[0]
    dff = w_out.shape[1]
    row = lambda i, te, na: (i, 0)
    in_specs = [
        pl.BlockSpec((tm, D_MODEL), row),
        pl.BlockSpec((1, D_MODEL, 2 * dff), lambda i, te, na: (te[i], 0, 0)),
        pl.BlockSpec((1, dff, D_MODEL), lambda i, te, na: (te[i], 0, 0)),
    ]
    args = [x, w_in, w_out]
    if ln is not None:
        in_specs += [pl.BlockSpec((1, D_MODEL), lambda i, te, na: (0, 0))] * 2
        args += [ln[0].reshape(1, D_MODEL), ln[1].reshape(1, D_MODEL)]
    return pl.pallas_call(
        functools.partial(_ffn_body, dff=dff, fused_ln=ln is not None),
        grid_spec=pltpu.PrefetchScalarGridSpec(
            num_scalar_prefetch=2,
            grid=(m // tm,),
            in_specs=in_specs,
            out_specs=pl.BlockSpec((tm, D_MODEL), row),
        ),
        out_shape=jax.ShapeDtypeStruct((m, D_MODEL), F32),
        compiler_params=_cparams(1),
        name=name,
    )(tile_expert, n_active, *args)


def _dense_ffn(x, w_in, w_out, layer, g, b, name):
    tm = _row_tile(x.shape[0])
    n_tiles = x.shape[0] // tm
    te = jnp.full((n_tiles,), layer, I32)
    return _ffn(x, w_in, w_out, te, jnp.full((1,), n_tiles, I32), tm, ln=(g, b), name=name)


def _router_body(x_ref, w_ref, b_ref, e_ref, p_ref):
    logits = jnp.dot(x_ref[...], w_ref[...], preferred_element_type=F32,
                     precision=lax.Precision.HIGHEST) + b_ref[...]
    lane = lax.broadcasted_iota(I32, logits.shape, 1).astype(F32)
    v1 = jnp.max(logits, axis=-1, keepdims=True)
    i1 = jnp.min(jnp.where(logits == v1, lane, float(LANES)), axis=-1, keepdims=True)
    rest = jnp.where(lane == i1, NEG_INF, logits)
    v2 = jnp.max(rest, axis=-1, keepdims=True)
    i2 = jnp.min(jnp.where(rest == v2, lane, float(LANES)), axis=-1, keepdims=True)
    e2 = jnp.exp(v2 - v1)
    den = 1.0 + e2
    col = lax.broadcasted_iota(I32, e_ref.shape, 1)
    e_ref[...] = jnp.where(col == 0, i1, i2).astype(I32)
    p_ref[...] = jnp.where(col == 0, 1.0 / den, e2 / den)


def _router(x, w_router, b_router):
    m = x.shape[0]
    tm = _row_tile(m)
    w_pad = jnp.zeros((D_MODEL, LANES), F32).at[:, :N_EXPERTS].set(w_router)
    b_pad = jnp.full((1, LANES), NEG_INF, F32).at[0, :N_EXPERTS].set(b_router)
    row = lambda i: (i, 0)
    return pl.pallas_call(
        _router_body,
        grid=(m // tm,),
        in_specs=[pl.BlockSpec((tm, D_MODEL), row), _resident((D_MODEL, LANES)), _resident((1, LANES))],
        out_specs=[pl.BlockSpec((tm, 2), row), pl.BlockSpec((tm, 2), row)],
        out_shape=[jax.ShapeDtypeStruct((m, 2), I32), jax.ShapeDtypeStruct((m, 2), F32)],
        compiler_params=_cparams(1),
        name="moe_router",
    )(x, w_pad, b_pad)


def _row_copy(src, src_row, dst, dst_row, sem):
    return pltpu.make_async_copy(src.at[pl.ds(src_row, 1)], dst.at[pl.ds(dst_row, 1)], sem)


def _dispatch_body(p0_ref, p1_ref, x_hbm, init_hbm, o_hbm, sem, *, tm):
    del init_hbm
    base = pl.program_id(0) * tm

    def issue(r, carry):
        _row_copy(x_hbm, base + r, o_hbm, p0_ref[0, 0, r], sem).start()
        _row_copy(x_hbm, base + r, o_hbm, p1_ref[0, 0, r], sem).start()
        return carry

    def drain(r, carry):
        _row_copy(x_hbm, 0, o_hbm, 0, sem).wait()
        _row_copy(x_hbm, 0, o_hbm, 0, sem).wait()
        return carry

    lax.fori_loop(0, tm, issue, 0)
    lax.fori_loop(0, tm, drain, 0)


def _dispatch(x, pos0, pos1, n_rows):
    m = x.shape[0]
    tm = _row_tile(m)
    smem = lambda: pl.BlockSpec((1, 1, tm), lambda i: (i, 0, 0), memory_space=pltpu.SMEM)
    return pl.pallas_call(
        functools.partial(_dispatch_body, tm=tm),
        grid=(m // tm,),
        in_specs=[smem(), smem(), pl.BlockSpec(memory_space=pl.ANY), pl.BlockSpec(memory_space=pl.ANY)],
        out_specs=pl.BlockSpec(memory_space=pl.ANY),
        out_shape=jax.ShapeDtypeStruct((n_rows, D_MODEL), F32),
        scratch_shapes=[pltpu.SemaphoreType.DMA(())],
        input_output_aliases={3: 0},
        compiler_params=_cparams(1),
        name="moe_dispatch",
    )(pos0.reshape(-1, 1, tm), pos1.reshape(-1, 1, tm), x, jnp.zeros((n_rows, D_MODEL), F32))


def _combine_body(p0_ref, p1_ref, y_hbm, pw_ref, x_ref, g_ref, b_ref, o_ref, buf, sem, *, tm):
    def issue(r, carry):
        _row_copy(y_hbm, p0_ref[0, 0, r], buf.at[0], r, sem).start()
        _row_copy(y_hbm, p1_ref[0, 0, r], buf.at[1], r, sem).start()
        return carry

    def drain(r, carry):
        _row_copy(y_hbm, 0, buf.at[0], 0, sem).wait()
        _row_copy(y_hbm, 0, buf.at[1], 0, sem).wait()
        return carry

    lax.fori_loop(0, tm, issue, 0)
    lax.fori_loop(0, tm, drain, 0)
    pw = pw_ref[...]
    y = pw[:, 0:1] * buf[0] + pw[:, 1:2] * buf[1]
    o_ref[...] = _layer_norm(ALPHA * x_ref[...] + y, g_ref[...], b_ref[...])


def _combine_ln(ys, pos0, pos1, pw, x, g, b):
    m = x.shape[0]
    tm = _row_tile(m)
    row = lambda i: (i, 0)
    smem = lambda: pl.BlockSpec((1, 1, tm), lambda i: (i, 0, 0), memory_space=pltpu.SMEM)
    return pl.pallas_call(
        functools.partial(_combine_body, tm=tm),
        grid=(m // tm,),
        in_specs=[smem(), smem(), pl.BlockSpec(memory_space=pl.ANY), pl.BlockSpec((tm, 2), row),
                  pl.BlockSpec((tm, D_MODEL), row), _resident((1, D_MODEL)), _resident((1, D_MODEL))],
        out_specs=pl.BlockSpec((tm, D_MODEL), row),
        out_shape=jax.ShapeDtypeStruct((m, D_MODEL), F32),
        scratch_shapes=[pltpu.VMEM((2, tm, D_MODEL), F32), pltpu.SemaphoreType.DMA(())],
        compiler_params=_cparams(1),
        name="moe_combine_ln",
    )(pos0.reshape(-1, 1, tm), pos1.reshape(-1, 1, tm), ys, pw, x, g.reshape(1, D_MODEL), b.reshape(1, D_MODEL))


def _moe(x, w_router, b_router, w_in, w_out, g, b):
    m = x.shape[0]
    tm = _row_tile(m)
    experts, probs = _router(x, w_router, b_router)
    e_flat = jnp.concatenate([experts[:, 0], experts[:, 1]])
    one_hot = (e_flat[:, None] == jnp.arange(N_EXPERTS, dtype=I32)[None, :]).astype(I32)
    csum = jnp.cumsum(one_hot, axis=0)
    rank = jnp.sum(one_hot * csum, axis=1) - 1
    tiles_per_expert = (csum[-1] + tm - 1) // tm
    tile_end = jnp.cumsum(tiles_per_expert)
    start = (tile_end - tiles_per_expert) * tm
    pos = jnp.sum(one_hot * start[None, :], axis=1) + rank
    pos0, pos1 = pos[:m], pos[m:]
    n_tiles = (2 * m) // tm + N_EXPERTS
    tile_expert = jnp.minimum(
        jnp.sum(jnp.arange(n_tiles, dtype=I32)[:, None] >= tile_end[None, :], axis=1), N_EXPERTS - 1
    ).astype(I32)
    n_active = tile_end[-1:].astype(I32)

    xs = _dispatch(x, pos0, pos1, n_tiles * tm)
    ys = _ffn(xs, w_in, w_out, tile_expert, n_active, tm, name="moe_ffn")
    return _combine_ln(ys, pos0, pos1, probs, x, g, b)


def _rope_table_body(f_ref, cos_ref, sin_ref, *, past):
    pos = past + lax.broadcasted_iota(I32, cos_ref.shape, 0)
    ang = pos.astype(F32) * f_ref[...]
    lane = lax.broadcasted_iota(I32, cos_ref.shape, 1)
    cos_ref[...] = jnp.cos(ang)
    sin_ref[...] = jnp.where(lane < MLA_NOPE_DIM + MLA_ROPE_DIM // 2, -jnp.sin(ang), jnp.sin(ang))


def _rope_tables(t, past):
    half = MLA_ROPE_DIM // 2
    inv_freq = ROPE_THETA ** (-jnp.arange(half, dtype=F32) / half)
    f = jnp.zeros((1, LANES), F32)
    f = f.at[0, MLA_NOPE_DIM:MLA_NOPE_DIM + half].set(inv_freq)
    f = f.at[0, MLA_NOPE_DIM + half:MLA_QK_DIM].set(inv_freq)
    return pl.pallas_call(
        functools.partial(_rope_table_body, past=past),
        out_shape=[jax.ShapeDtypeStruct((t, LANES), F32)] * 2,
        name="rope_table",
    )(f)


def _rope_block(z, cosv, sinv):
    lane = lax.broadcasted_iota(I32, z.shape, 1)
    half = MLA_ROPE_DIM // 2
    partner = jnp.where(lane < MLA_NOPE_DIM + half,
                        pltpu.roll(z, LANES - half, 1), pltpu.roll(z, half, 1))
    return z * cosv + partner * sinv


def _mla_q_body(x_ref, wdq_ref, gq_ref, wuq_ref, cos_ref, sin_ref, o_ref):
    cq = _rms(_dot(x_ref[...].astype(BF16), wdq_ref[...]), gq_ref[...]).astype(BF16)
    cosv = cos_ref[...]
    sinv = sin_ref[...]
    width = 4 * LANES
    for c0 in range(0, MLA_HEADS * LANES, width):
        r = _dot(cq, wuq_ref[:, c0:c0 + width])
        for j in range(0, width, LANES):
            o_ref[:, c0 + j:c0 + j + LANES] = _rope_block(r[:, j:j + LANES], cosv, sinv).astype(BF16)


def _pos_tables(tables, m, t):
    tm = _row_tile(m)
    if tm <= t:
        assert t % tm == 0
        n = t // tm
        return tables, pl.BlockSpec((tm, LANES), lambda i: (i % n, 0))
    assert tm % t == 0
    return [jnp.tile(tb, (tm // t, 1)) for tb in tables], pl.BlockSpec((tm, LANES), lambda i: (0, 0))


def _mla_q(x, w_dq, g_q, w_uq_pad, tables, t):
    m = x.shape[0]
    tm = _row_tile(m)
    (cos_t, sin_t), tspec = _pos_tables(tables, m, t)
    row = lambda i: (i, 0)
    return pl.pallas_call(
        _mla_q_body,
        grid=(m // tm,),
        in_specs=[pl.BlockSpec((tm, D_MODEL), row), _resident(w_dq.shape), _resident((1, MLA_Q_RANK)),
                  _resident(w_uq_pad.shape), tspec, tspec],
        out_specs=pl.BlockSpec((tm, MLA_HEADS * LANES), row),
        out_shape=jax.ShapeDtypeStruct((m, MLA_HEADS * LANES), BF16),
        compiler_params=_cparams(1),
        name="mla_q",
    )(x, w_dq, g_q.reshape(1, MLA_Q_RANK), w_uq_pad, cos_t, sin_t)


def _mla_kv_down_body(x_ref, w_ref, g_ref, cos_ref, sin_ref, ckv_ref, kr_ref):
    kv = _dot(x_ref[...].astype(BF16), w_ref[...])
    ckv_ref[...] = _rms(kv[:, :MLA_KV_RANK], g_ref[...])
    kr_ref[...] = _rope_block(kv[:, MLA_KV_RANK:], cos_ref[...], sin_ref[...])


def _mla_kv_down(x, w_dkv_pad, g_kv, tables, t):
    m = x.shape[0]
    tm = _row_tile(m)
    (cos_t, sin_t), tspec = _pos_tables(tables, m, t)
    row = lambda i: (i, 0)
    return pl.pallas_call(
        _mla_kv_down_body,
        grid=(m // tm,),
        in_specs=[pl.BlockSpec((tm, D_MODEL), row), _resident(w_dkv_pad.shape),
                  _resident((1, MLA_KV_RANK)), tspec, tspec],
        out_specs=[pl.BlockSpec((tm, MLA_KV_RANK), row), pl.BlockSpec((tm, LANES), row)],
        out_shape=[jax.ShapeDtypeStruct((m, MLA_KV_RANK), F32), jax.ShapeDtypeStruct((m, LANES), F32)],
        compiler_params=_cparams(1),
        name="mla_kv_down",
    )(x, w_dkv_pad, g_kv.reshape(1, MLA_KV_RANK), cos_t, sin_t)


def _mla_kv_up_body(c_ref, kr_ref, wuk_ref, wuv_ref, k_ref, v_ref):
    cb = c_ref[...].astype(BF16)
    kr = kr_ref[...]
    width = 4 * LANES
    for c0 in range(0, MLA_HEADS * LANES, width):
        r = _dot(cb, wuk_ref[:, c0:c0 + width])
        for j in range(0, width, LANES):
            k_ref[:, c0 + j:c0 + j + LANES] = (r[:, j:j + LANES] + kr).astype(BF16)
    for c0 in range(0, MLA_HEADS * MLA_V_DIM, width):
        v_ref[:, c0:c0 + width] = _dot(cb, wuv_ref[:, c0:c0 + width]).astype(BF16)


def _mla_kv_up(ckv, kr_pad, w_uk_pad, w_uv):
    m = ckv.shape[0]
    tm = ROW_TILE if m % ROW_TILE == 0 else (LANES // 8 if m % LANES else LANES)
    assert m % tm == 0
    row = lambda i: (i, 0)
    return pl.pallas_call(
        _mla_kv_up_body,
        grid=(m // tm,),
        in_specs=[pl.BlockSpec((tm, MLA_KV_RANK), row), pl.BlockSpec((tm, LANES), row),
                  _resident(w_uk_pad.shape), _resident(w_uv.shape)],
        out_specs=[pl.BlockSpec((tm, MLA_HEADS * LANES), row), pl.BlockSpec((tm, MLA_HEADS * MLA_V_DIM), row)],
        out_shape=[jax.ShapeDtypeStruct((m, MLA_HEADS * LANES), BF16),
                   jax.ShapeDtypeStruct((m, MLA_HEADS * MLA_V_DIM), BF16)],
        compiler_params=_cparams(1),
        name="mla_kv_up",
    )(ckv, kr_pad, w_uk_pad, w_uv)


def _pool_body(prev_ref, halo_ref, x_ref, wp_ref, sc_ref, g_ref, b_ref, o_ref, ext, *, tm, pos0):
    i = pl.program_id(1)
    x = x_ref[0]
    ext[HALO:, :] = x

    @pl.when(i == 0)
    def _():
        ext[:HALO, :] = prev_ref[0]

    @pl.when(i > 0)
    def _():
        ext[:HALO, :] = halo_ref[0]

    pos = pos0 + i * tm + lax.broadcasted_iota(I32, (tm, 1), 0)
    for gi, w in enumerate(POOL_WINDOWS):
        ch = slice(gi * POOL_GROUP, (gi + 1) * POOL_GROUP)
        win = x[:, ch]
        for j in range(1, w):
            win = win + ext[HALO - j:HALO - j + tm, ch]
        count = jnp.minimum(pos + 1, w).astype(F32)
        mixed = (win / count - x[:, ch]).astype(BF16)
        o_ref[0, :, ch] = _dot(mixed, wp_ref[gi])
    y = o_ref[0] * sc_ref[...]
    o_ref[0] = _layer_norm(ALPHA * x + y, g_ref[...], b_ref[...])


def _pool_ln(x, prev_rows, pos0, w_pool, scale, g, b):
    bsz, t, _ = x.shape
    tm = _row_tile(t)
    prev = jnp.concatenate([jnp.zeros((bsz, HALO - POOL_STATE, D_MODEL), F32), prev_rows], axis=1)
    per = tm // HALO
    tile = lambda b_, i: (b_, i, 0)
    vec = pl.BlockSpec((1, D_MODEL), lambda b_, i: (0, 0))
    return pl.pallas_call(
        functools.partial(_pool_body, tm=tm, pos0=pos0),
        grid=(bsz, t // tm),
        in_specs=[pl.BlockSpec((1, HALO, D_MODEL), lambda b_, i: (b_, 0, 0)),
                  pl.BlockSpec((1, HALO, D_MODEL), lambda b_, i: (b_, jnp.maximum(i * per - 1, 0), 0)),
                  pl.BlockSpec((1, tm, D_MODEL), tile),
                  pl.BlockSpec(w_pool.shape, lambda b_, i: (0, 0, 0)), vec, vec, vec],
        out_specs=pl.BlockSpec((1, tm, D_MODEL), tile),
        out_shape=jax.ShapeDtypeStruct((bsz, t, D_MODEL), F32),
        scratch_shapes=[pltpu.VMEM((tm + HALO, D_MODEL), F32)],
        compiler_params=_cparams(2),
        name="pool_ln",
    )(prev, x, x, w_pool, scale.reshape(1, D_MODEL), g.reshape(1, D_MODEL), b.reshape(1, D_MODEL))


def _gla_proj_body(x_ref, win_ref, wa1_ref, wa2_ref, ba_ref, q_ref, k_ref, v_ref, gt_ref, la_ref):
    xb = x_ref[...].astype(BF16)
    width = 2 * MXU_DIM
    for o_ref, c0, c1 in ((q_ref, 0, GLA_KEY), (k_ref, GLA_KEY, 2 * GLA_KEY),
                          (v_ref, 2 * GLA_KEY, 2 * GLA_KEY + GLA_VAL),
                          (gt_ref, 2 * GLA_KEY + GLA_VAL, 2 * GLA_KEY + 2 * GLA_VAL)):
        for j0 in range(c0, c1, width):
            o_ref[:, j0 - c0:j0 - c0 + width] = _dot(xb, win_ref[:, j0:j0 + width]).astype(o_ref.dtype)
    low = _dot(xb, wa1_ref[...]).astype(BF16)
    z = _dot(low, wa2_ref[...]) + ba_ref[...]
    la_ref[...] = (jnp.minimum(z, 0.0) - jnp.log(1.0 + jnp.exp(-jnp.abs(z)))) / GLA_TAU


def _gla_proj(x, w_in, w_a1_pad, w_a2_pad, b_a):
    m = x.shape[0]
    tm = _row_tile(m)
    row = lambda i: (i, 0)
    widths = (GLA_KEY, GLA_KEY, GLA_VAL, GLA_VAL, GLA_KEY)
    dtypes = (F32, F32, BF16, F32, F32)
    return pl.pallas_call(
        _gla_proj_body,
        grid=(m // tm,),
        in_specs=[pl.BlockSpec((tm, D_MODEL), row), _resident(w_in.shape), _resident(w_a1_pad.shape),
                  _resident(w_a2_pad.shape), _resident((1, GLA_KEY))],
        out_specs=[pl.BlockSpec((tm, w), row) for w in widths],
        out_shape=[jax.ShapeDtypeStruct((m, w), d) for w, d in zip(widths, dtypes)],
        compiler_params=_cparams(1),
        name="gla_proj",
    )(x, w_in, w_a1_pad, w_a2_pad, b_a.reshape(1, GLA_KEY))


def _cumsum_rows(z):
    n = z.shape[0]
    row = lax.broadcasted_iota(I32, z.shape, 0)
    shift = 1
    while shift < n:
        z = z + jnp.where(row >= shift, pltpu.roll(z, shift, 0), 0.0)
        shift *= 2
    return z


def _gla_body(q_ref, k_ref, la_ref, v_ref, gt_ref, s0_ref, hg_ref, o_ref, s_ref, *, blk, n_blk):
    s_ref[0, 0] = s0_ref[0, 0]
    row = lax.broadcasted_iota(I32, (blk, blk), 0)
    col = lax.broadcasted_iota(I32, (blk, blk), 1)
    causal = row >= col

    def block(n, carry):
        r0 = pl.multiple_of(n * blk, blk)
        rows = pl.ds(r0, blk)
        cum = _cumsum_rows(la_ref[0, rows, :])
        cum_last = cum[blk - 1:blk, :]
        q = q_ref[0, rows, :] * (GLA_DK ** -0.5)
        k = k_ref[0, rows, :]
        v = v_ref[0, rows, :]
        q_dec = (q * jnp.exp(cum)).astype(BF16)
        k_inv = (k * jnp.exp(-cum)).astype(BF16)
        k_end = (k * jnp.exp(cum_last - cum)).astype(BF16)
        scores = jnp.where(causal, _dot_nt(q_dec, k_inv), 0.0).astype(BF16)
        state = s_ref[0, 0]
        o = _dot(scores, v) + _dot_nt(q_dec, state.astype(BF16))
        s_ref[0, 0] = jnp.exp(cum_last) * state + _dot_tn(v, k_end)
        gate = gt_ref[0, rows, :]
        o_ref[0, rows, :] = (_rms(o, hg_ref[...]) * jax.nn.silu(gate)).astype(BF16)
        return carry

    lax.fori_loop(0, n_blk, block, 0)


def _gla(q, k, la, v, gate, s0_t, head_g):
    bsz, t, _ = q.shape
    blk = GLA_BLOCK if t % GLA_BLOCK == 0 else t
    kspec = pl.BlockSpec((1, t, GLA_DK), lambda b_, h: (b_, 0, h))
    vspec = pl.BlockSpec((1, t, GLA_DV), lambda b_, h: (b_, 0, h))
    sspec = pl.BlockSpec((1, 1, GLA_DV, GLA_DK), lambda b_, h: (b_, h, 0, 0))
    return pl.pallas_call(
        functools.partial(_gla_body, blk=blk, n_blk=t // blk),
        grid=(bsz, GLA_HEADS),
        in_specs=[kspec, kspec, kspec, vspec, vspec, sspec, pl.BlockSpec((1, GLA_DV), lambda b_, h: (0, 0))],
        out_specs=[vspec, sspec],
        out_shape=[jax.ShapeDtypeStruct((bsz, t, GLA_VAL), BF16),
                   jax.ShapeDtypeStruct((bsz, GLA_HEADS, GLA_DV, GLA_DK), F32)],
        compiler_params=_cparams(2),
        name="gla",
    )(q, k, la, v, gate, s0_t, head_g.reshape(1, GLA_DV))


def _pad_head_cols(w, heads, width):
    k = w.shape[0]
    w = w.reshape(k, heads, width)
    return jnp.pad(w, ((0, 0), (0, 0), (0, LANES - width))).reshape(k, heads * LANES)


def kernel(x_prompt, x_sample, cache_a_k, cache_a_v, cache_b_ckv, cache_b_krope, state_c_rows, state_d_gla, a_w_qkv, a_lam_q1, a_lam_k1, a_lam_q2, a_lam_k2, a_subln_g, a_w_o, b_w_dq, b_q_norm_g, b_w_uq, b_w_dkv, b_kv_norm_g, b_w_uk, b_w_uv, b_w_o, c_w_pool, c_scale, d_w_in, d_w_a1, d_w_a2, d_b_a, d_head_g, d_w_o, ln_g, ln_b, ffn_w_in, ffn_w_out, moe_w_router, moe_b_router, moe_w_in, moe_w_out):
    bf = lambda w: w.astype(BF16)
    d = D_MODEL
    past = cache_a_k.shape[1]
    streams = [(x_prompt.shape[0], x_prompt.shape[1], 0), (x_sample.shape[0], x_sample.shape[1], past)]
    xs = [x_prompt.reshape(-1, d), x_sample.reshape(-1, d)]

    ffn_in, ffn_out = bf(ffn_w_in), bf(ffn_w_out)
    moe_in, moe_out = bf(moe_w_in), bf(moe_w_out)

    w_qkv, w_o = bf(a_w_qkv), bf(a_w_o)
    lam_vecs = jnp.stack([a_lam_q1, a_lam_k1, a_lam_q2, a_lam_k2]).astype(F32)
    lambda_init = 0.8 - 0.6 * math.exp(-0.3 * 0)
    caches = [(None, None), (cache_a_k, cache_a_v)]
    a_k, a_v = [], []
    for s, ((bsz, t, p), (ck, cv)) in enumerate(zip(streams, caches)):
        qb, kf, kb, vf, vb = _proj(
            xs[s], w_qkv,
            [(0, d, [(BF16, DA_HEAD_DIM ** -0.5)]), (d, 2 * d, [(F32, 1.0), (BF16, 1.0)]),
             (2 * d, 3 * d, [(F32, 1.0), (BF16, 1.0)])], name="a_qkv")
        kb, vb = kb.reshape(bsz, t, d), vb.reshape(bsz, t, d)
        if p:
            kb = jnp.concatenate([bf(ck.reshape(bsz, p, d)), kb], axis=1)
            vb = jnp.concatenate([bf(cv.reshape(bsz, p, d)), vb], axis=1)
        o = _attention("diff", qb.reshape(bsz, t, d), kb, vb, lam_vecs=lam_vecs, subln_g=a_subln_g,
                       lambda_init=lambda_init, name="a_attn")
        xs[s] = _oproj_ln(o.reshape(-1, d), w_o, xs[s], ln_g[0, 0], ln_b[0, 0], name="a_out_ln")
        xs[s] = _dense_ffn(xs[s], ffn_in, ffn_out, 0, ln_g[0, 1], ln_b[0, 1], name="ffn0")
        a_k.append(kf.reshape(bsz, t, DA_HEADS, 2, DA_HEAD_DIM))
        a_v.append(vf.reshape(bsz, t, DA_HEADS, 2 * DA_HEAD_DIM))

    w_dq, w_o = bf(b_w_dq), bf(b_w_o)
    w_uq_pad = bf(_pad_head_cols(b_w_uq, MLA_HEADS, MLA_QK_DIM))
    w_uk_pad = bf(_pad_head_cols(b_w_uk, MLA_HEADS, MLA_NOPE_DIM))
    w_uv = bf(b_w_uv)
    w_dkv_pad = bf(jnp.concatenate(
        [b_w_dkv[:, :MLA_KV_RANK], jnp.zeros((d, MLA_NOPE_DIM), F32), b_w_dkv[:, MLA_KV_RANK:],
         jnp.zeros((d, LANES - MLA_QK_DIM), F32)], axis=1))
    caches = [(None, None), (cache_b_ckv, cache_b_krope)]
    b_ckv, b_kr = [], []
    for s, ((bsz, t, p), (cc, ckr)) in enumerate(zip(streams, caches)):
        tables = _rope_tables(t, p)
        q = _mla_q(xs[s], w_dq, b_q_norm_g, w_uq_pad, tables, t)
        ckv, kr_pad = _mla_kv_down(xs[s], w_dkv_pad, b_kv_norm_g, tables, t)
        ckv_all, kr_all = ckv, kr_pad
        if p:
            ckr_pad = jnp.pad(ckr, ((0, 0), (0, 0), (MLA_NOPE_DIM, LANES - MLA_QK_DIM)))
            ckv_all = jnp.concatenate([cc, ckv.reshape(bsz, t, -1)], axis=1).reshape(-1, MLA_KV_RANK)
            kr_all = jnp.concatenate([ckr_pad, kr_pad.reshape(bsz, t, -1)], axis=1).reshape(-1, LANES)
        k_pad, v = _mla_kv_up(ckv_all, kr_all, w_uk_pad, w_uv)
        o = _attention("pair", q.reshape(bsz, t, -1), k_pad.reshape(bsz, p + t, -1),
                       v.reshape(bsz, p + t, -1), scale=MLA_QK_DIM ** -0.5, name="b_attn")
        xs[s] = _oproj_ln(o.reshape(-1, d), w_o, xs[s], ln_g[1, 0], ln_b[1, 0], name="b_out_ln")
        xs[s] = _moe(xs[s], moe_w_router[0], moe_b_router[0], moe_in[0], moe_out[0], ln_g[1, 1], ln_b[1, 1])
        b_ckv.append(ckv.reshape(bsz, t, MLA_KV_RANK))
        b_kr.append(kr_pad[:, MLA_NOPE_DIM:MLA_QK_DIM].reshape(bsz, t, MLA_ROPE_DIM))

    w_pool = bf(c_w_pool)
    prevs = [jnp.zeros((streams[0][0], POOL_STATE, d), F32), state_c_rows]
    c_rows = []
    for s, ((bsz, t, p), prev) in enumerate(zip(streams, prevs)):
        x3 = xs[s].reshape(bsz, t, d)
        c_rows.append(jnp.concatenate([prev, x3], axis=1)[:, -POOL_STATE:])
        xs[s] = _pool_ln(x3, prev, p, w_pool, c_scale, ln_g[2, 0], ln_b[2, 0]).reshape(-1, d)
        xs[s] = _dense_ffn(xs[s], ffn_in, ffn_out, 1, ln_g[2, 1], ln_b[2, 1], name="ffn1")

    w_in, w_o = bf(d_w_in), bf(d_w_o)
    w_a1_pad = bf(jnp.pad(d_w_a1, ((0, 0), (0, LANES - GLA_GATE_RANK))))
    w_a2_pad = bf(jnp.pad(d_w_a2, ((0, LANES - GLA_GATE_RANK), (0, 0))))
    states = [jnp.zeros((streams[0][0], GLA_HEADS, GLA_DK, GLA_DV), F32), state_d_gla]
    d_gla = []
    for s, ((bsz, t, p), s0) in enumerate(zip(streams, states)):
        q, k, v, gate, la = _gla_proj(xs[s], w_in, w_a1_pad, w_a2_pad, d_b_a)
        shp = lambda z: z.reshape(bsz, t, -1)
        o, s_t = _gla(shp(q), shp(k), shp(la), shp(v), shp(gate), jnp.swapaxes(s0, 2, 3), d_head_g)
        xs[s] = _oproj_ln(o.reshape(-1, d), w_o, xs[s], ln_g[3, 0], ln_b[3, 0], name="d_out_ln")
        xs[s] = _moe(xs[s], moe_w_router[1], moe_b_router[1], moe_in[1], moe_out[1], ln_g[3, 1], ln_b[3, 1])
        d_gla.append(jnp.swapaxes(s_t, 2, 3))

    y = [xs[s].reshape(bsz, t, d) for s, (bsz, t, _) in enumerate(streams)]
    return (y[0], y[1], a_k[0], a_v[0], a_k[1], a_v[1], b_ckv[0], b_kr[0], b_ckv[1], b_kr[1],
            c_rows[0], c_rows[1], d_gla[0], d_gla[1])
```

```python
import functools
import math

import jax
import jax.numpy as jnp
from jax import lax
from jax.experimental import pallas as pl
from jax.experimental.pallas import tpu as pltpu

F32 = jnp.float32
BF16 = jnp.bfloat16
I32 = jnp.int32

D_MODEL = 1024
DEPTH = 4
CHUNK = 64
ALPHA = (2 * DEPTH) ** 0.25
LN_EPS = 1e-5
RMS_EPS = 1e-6
NEG_INF = -1e30

DA_HEADS = 8
DA_HEAD_DIM = 64
MLA_HEADS = 16
MLA_NOPE_DIM = 64
MLA_ROPE_DIM = 32
MLA_QK_DIM = MLA_NOPE_DIM + MLA_ROPE_DIM
MLA_V_DIM = 64
MLA_Q_RANK = 384
MLA_KV_RANK = 256
ROPE_THETA = 10000.0
POOL_WINDOWS = (2, 4, 8, 16)
POOL_GROUP = D_MODEL // len(POOL_WINDOWS)
POOL_STATE = max(POOL_WINDOWS) - 1
GLA_HEADS = 4
GLA_DK = 128
GLA_DV = 256
GLA_KEY = GLA_HEADS * GLA_DK
GLA_VAL = GLA_HEADS * GLA_DV
GLA_GATE_RANK = 16
GLA_TAU = 16.0
GLA_BLOCK = CHUNK
N_EXPERTS = 8

LANES = 128
MXU_DIM = 256
VMEM_LIMIT_BYTES = 56 * 1024 * 1024
ROW_TILE = 512
HALO = 16


def _cparams(n_axes):
    return pltpu.CompilerParams(
        dimension_semantics=("arbitrary",) * n_axes, vmem_limit_bytes=VMEM_LIMIT_BYTES)


def _row_tile(m):
    return ROW_TILE if m % ROW_TILE == 0 else m


def _resident(shape):
    nd = len(shape)
    return pl.BlockSpec(shape, lambda *_: (0,) * nd)


def _layer_norm(z, g, b):
    mu = jnp.mean(z, axis=-1, keepdims=True)
    zc = z - mu
    var = jnp.mean(zc * zc, axis=-1, keepdims=True)
    return zc * lax.rsqrt(var + LN_EPS) * g + b


def _rms(z, g):
    return z * lax.rsqrt(jnp.mean(z * z, axis=-1, keepdims=True) + RMS_EPS) * g


def _dot(a, b):
    return jnp.dot(a, b, preferred_element_type=F32)


def _dot_nt(a, b):
    return lax.dot_general(a, b, (((1,), (1,)), ((), ())), preferred_element_type=F32)


def _dot_tn(a, b):
    return lax.dot_general(a, b, (((0,), (0,)), ((), ())), preferred_element_type=F32)


def _proj_body(x_ref, w_ref, *o_refs, groups, chunk):
    xb = x_ref[...].astype(BF16)
    o_refs = iter(o_refs)
    for c0, c1, sinks in groups:
        refs = [next(o_refs) for _ in sinks]
        for j0 in range(c0, c1, chunk):
            j1 = min(j0 + chunk, c1)
            r = _dot(xb, w_ref[:, j0:j1])
            for o_ref, (_, sc) in zip(refs, sinks):
                o_ref[:, j0 - c0:j1 - c0] = (r if sc == 1.0 else r * sc).astype(o_ref.dtype)


def _proj(x, w, groups, name):
    m, k = x.shape
    tm = _row_tile(m)
    outs = [(c1 - c0, dt) for c0, c1, sinks in groups for dt, _ in sinks]
    return pl.pallas_call(
        functools.partial(_proj_body, groups=groups, chunk=2 * MXU_DIM),
        grid=(m // tm,),
        in_specs=[pl.BlockSpec((tm, k), lambda i: (i, 0)), _resident(w.shape)],
        out_specs=[pl.BlockSpec((tm, wd), lambda i: (i, 0)) for wd, _ in outs],
        out_shape=[jax.ShapeDtypeStruct((m, wd), dt) for wd, dt in outs],
        compiler_params=_cparams(1),
        name=name,
    )(x, w)


def _oproj_ln_body(a_ref, w_ref, x_ref, g_ref, b_ref, o_ref):
    y = _dot(a_ref[...], w_ref[...])
    o_ref[...] = _layer_norm(ALPHA * x_ref[...] + y, g_ref[...], b_ref[...])


def _oproj_ln(a, w, x, g, b, name):
    m, kd = a.shape
    tm = _row_tile(m)
    row = lambda i: (i, 0)
    return pl.pallas_call(
        _oproj_ln_body,
        grid=(m // tm,),
        in_specs=[pl.BlockSpec((tm, kd), row), _resident(w.shape), pl.BlockSpec((tm, D_MODEL), row),
                  _resident((1, D_MODEL)), _resident((1, D_MODEL))],
        out_specs=pl.BlockSpec((tm, D_MODEL), row),
        out_shape=jax.ShapeDtypeStruct((m, D_MODEL), F32),
        compiler_params=_cparams(1),
        name=name,
    )(a, w, x, g.reshape(1, D_MODEL), b.reshape(1, D_MODEL))


def _attn_plan(tq_total, tk_valid):
    past = tk_valid - tq_total
    if past == 0:
        t = min(ROW_TILE, tq_total)
        assert tq_total % t == 0 and t % LANES == 0
        return t, t, tq_total // t, None, t, tk_valid
    tq = tq_total
    first_chunk = past // CHUNK
    all_visible = min((first_chunk + 1) * CHUNK, tk_valid)
    tk = ROW_TILE
    n_full = all_visible // tk
    last_chunk = (past + tq - 1) // CHUNK
    end = min((last_chunk + 1) * CHUNK, tk_valid)
    tail = -(-(end - n_full * tk) // LANES) * LANES
    return tq, tk, 1, n_full, tail, max(n_full * tk + tail, tk_valid)


def _attn_body(*refs, mode, tq, tk, n_full_static, tail, past, tk_valid, scale, lambda_init):
    if mode == "diff":
        lam_ref, q_ref, k_ref, v_ref, g_ref, o_ref, m_sc, l_sc, acc_sc = refs
    else:
        q_ref, k_ref, v_ref, o_ref, m_sc, l_sc, acc_sc = refs
    qi = pl.program_id(2)
    n_full = qi if n_full_static is None else n_full_static
    rows = 2 * tq
    lane = lax.broadcasted_iota(I32, (rows, LANES), 1)
    lo = lax.broadcasted_iota(I32, (tq, LANES), 1) < LANES // 2
    lo_row = lax.broadcasted_iota(I32, (1, LANES), 1) < LANES // 2

    q = q_ref[0]
    if mode == "diff":
        zero = jnp.zeros_like(q)
        q_both = jnp.concatenate([jnp.where(lo, q, zero), jnp.where(lo, zero, q)], axis=0)
    else:
        q0 = q[:, :LANES]
        q1 = q[:, LANES:]

    m_sc[...] = jnp.full(m_sc.shape, NEG_INF, F32)
    l_sc[...] = jnp.zeros(l_sc.shape, F32)
    acc_sc[...] = jnp.zeros(acc_sc.shape, F32)

    def tile(off, length, masked):
        k = k_ref[0, pl.ds(off, length), :]
        v = v_ref[0, pl.ds(off, length), :]
        if mode == "diff":
            s = _dot_nt(q_both, k)
        else:
            s = jnp.concatenate([_dot_nt(q0, k[:, :LANES]), _dot_nt(q1, k[:, LANES:])], axis=0)
            s = s * scale
        blocks = [s[:, c:c + LANES] for c in range(0, length, LANES)]
        if masked:
            row = lax.broadcasted_iota(I32, (rows, LANES), 0)
            q_pos = past + qi * tq + jnp.where(row >= tq, row - tq, row)
            limit = jnp.minimum((q_pos // CHUNK + 1) * CHUNK, tk_valid) - off
            blocks = [jnp.where(lane + c * LANES < limit, blk, NEG_INF) for c, blk in enumerate(blocks)]
        m_old = m_sc[...]
        m_new = jnp.maximum(
            m_old, jnp.max(functools.reduce(jnp.maximum, blocks), axis=-1, keepdims=True))
        alpha = jnp.exp(m_old - m_new)
        probs = [jnp.exp(blk - m_new) for blk in blocks]
        l_sc[...] = alpha * l_sc[...] + functools.reduce(jnp.add, probs)
        m_sc[...] = m_new
        pb = jnp.concatenate([p.astype(BF16) for p in probs], axis=1)
        if mode == "diff":
            acc_sc[...] = alpha * acc_sc[...] + _dot(pb, v)
        else:
            zero = jnp.zeros_like(v)
            v0 = jnp.where(lo_row, v, zero)
            v1 = jnp.where(lo_row, zero, v)
            alpha_l = jnp.where(lo, alpha[:tq], alpha[tq:])
            acc_sc[...] = alpha_l * acc_sc[...] + _dot(pb[:tq], v0) + _dot(pb[tq:], v1)

    def full_tile(j, carry):
        tile(pl.multiple_of(j * tk, tk), tk, False)
        return carry

    lax.fori_loop(0, n_full, full_tile, 0)
    tile(n_full * tk if n_full_static is not None else pl.multiple_of(n_full * tk, tk), tail, True)

    inv_l = 1.0 / jnp.sum(l_sc[...], axis=-1, keepdims=True)
    if mode == "diff":
        lam_v = lam_ref[...]
        lam = (jnp.exp(jnp.sum(lam_v[0:1] * lam_v[1:2], keepdims=True))
               - jnp.exp(jnp.sum(lam_v[2:3] * lam_v[3:4], keepdims=True)) + lambda_init)
        acc = acc_sc[...]
        o = acc[:tq] * inv_l[:tq] - lam * (acc[tq:] * inv_l[tq:])
        o = _rms(o, g_ref[...]) * (1.0 - lambda_init)
    else:
        o = acc_sc[...] * jnp.where(lo, inv_l[:tq], inv_l[tq:])
    o_ref[0] = o.astype(o_ref.dtype)


def _attention(mode, q, k, v, *, lam_vecs=None, subln_g=None, lambda_init=0.0, scale=1.0, name):
    bsz, tq_total, _ = q.shape
    tk_valid = k.shape[1]
    qw = LANES if mode == "diff" else 2 * LANES
    heads = q.shape[2] // qw
    tq, tk, nq, n_full, tail, tk_pad = _attn_plan(tq_total, tk_valid)
    if tk_pad > tk_valid:
        k = jnp.pad(k, ((0, 0), (0, tk_pad - tk_valid), (0, 0)))
        v = jnp.pad(v, ((0, 0), (0, tk_pad - tk_valid), (0, 0)))
    body = functools.partial(
        _attn_body, mode=mode, tq=tq, tk=tk, n_full_static=n_full, tail=tail,
        past=tk_valid - tq_total, tk_valid=tk_valid, scale=scale, lambda_init=lambda_init)
    in_specs = [
        pl.BlockSpec((1, tq, qw), lambda b, h, i: (b, i, h)),
        pl.BlockSpec((1, tk_pad, qw), lambda b, h, i: (b, 0, h)),
        pl.BlockSpec((1, tk_pad, LANES), lambda b, h, i: (b, 0, h)),
    ]
    args = [q, k, v]
    if mode == "diff":
        in_specs = [_resident((4, DA_HEAD_DIM))] + in_specs + [_resident((1, LANES))]
        args = [lam_vecs] + args + [subln_g.reshape(1, LANES)]
    return pl.pallas_call(
        body,
        grid=(bsz, heads, nq),
        in_specs=in_specs,
        out_specs=pl.BlockSpec((1, tq, LANES), lambda b, h, i: (b, i, h)),
        out_shape=jax.ShapeDtypeStruct((bsz, tq_total, heads * LANES), BF16),
        scratch_shapes=[pltpu.VMEM((2 * tq, LANES), F32), pltpu.VMEM((2 * tq, LANES), F32),
                        pltpu.VMEM((2 * tq if mode == "diff" else tq, LANES), F32)],
        compiler_params=_cparams(3),
        name=name,
    )(*args)


def _ffn_body(te_ref, na_ref, x_ref, win_ref, wout_ref, *rest, dff, fused_ln):
    if fused_ln:
        g_ref, b_ref, o_ref = rest
    else:
        (o_ref,) = rest
    i = pl.program_id(0)

    @pl.when(i < na_ref[0])
    def _():
        x = x_ref[...]
        xb = x.astype(BF16)
        acc = jnp.zeros((x.shape[0], D_MODEL), F32)
        for c0 in range(0, dff, MXU_DIM):
            c1 = min(c0 + MXU_DIM, dff)
            gate = _dot(xb, win_ref[0, :, c0:c1])
            up = _dot(xb, win_ref[0, :, dff + c0:dff + c1])
            act = (jax.nn.silu(gate) * up).astype(BF16)
            acc = acc + _dot(act, wout_ref[0, c0:c1, :])
        if fused_ln:
            acc = _layer_norm(ALPHA * x + acc, g_ref[...], b_ref[...])
        o_ref[...] = acc

    @pl.when(i >= na_ref[0])
    def _():
        o_ref[...] = jnp.zeros(o_ref.shape, F32)


def _ffn(x, w_in, w_out, tile_expert, n_active, tm, ln=None, name="ffn"):
    m = x.shape[0]
    dff = w_out.shape[1]
    row = lambda i, te, na: (i, 0)
    in_specs = [
        pl.BlockSpec((tm, D_MODEL), row),
        pl.BlockSpec((1, D_MODEL, 2 * dff), lambda i, te, na: (te[i], 0, 0)),
        pl.BlockSpec((1, dff, D_MODEL), lambda i, te, na: (te[i], 0, 0)),
    ]
    args = [x, w_in, w_out]
    if ln is not None:
        in_specs += [pl.BlockSpec((1, D_MODEL), lambda i, te, na: (0, 0))] * 2
        args += [ln[0].reshape(1, D_MODEL), ln[1].reshape(1, D_MODEL)]
    return pl.pallas_call(
        functools.partial(_ffn_body, dff=dff, fused_ln=ln is not None),
        grid_spec=pltpu.PrefetchScalarGridSpec(
            num_scalar_prefetch=2,
            grid=(m // tm,),
            in_specs=in_specs,
            out_specs=pl.BlockSpec((tm, D_MODEL), row),
        ),
        out_shape=jax.ShapeDtypeStruct((m, D_MODEL), F32),
        compiler_params=_cparams(1),
        name=name,
    )(tile_expert, n_active, *args)


def _dense_ffn(x, w_in, w_out, layer, g, b, name):
    tm = _row_tile(x.shape[0])
    n_tiles = x.shape[0] // tm
    te = jnp.full((n_tiles,), layer, I32)
    return _ffn(x, w_in, w_out, te, jnp.full((1,), n_tiles, I32), tm, ln=(g, b), name=name)


def _router_body(x_ref, w_ref, b_ref, e_ref, p_ref):
    logits = jnp.dot(x_ref[...], w_ref[...], preferred_element_type=F32,
                     precision=lax.Precision.HIGHEST) + b_ref[...]
    lane = lax.broadcasted_iota(I32, logits.shape, 1).astype(F32)
    v1 = jnp.max(logits, axis=-1, keepdims=True)
    i1 = jnp.min(jnp.where(logits == v1, lane, float(LANES)), axis=-1, keepdims=True)
    rest = jnp.where(lane == i1, NEG_INF, logits)
    v2 = jnp.max(rest, axis=-1, keepdims=True)
    i2 = jnp.min(jnp.where(rest == v2, lane, float(LANES)), axis=-1, keepdims=True)
    e2 = jnp.exp(v2 - v1)
    den = 1.0 + e2
    col = lax.broadcasted_iota(I32, e_ref.shape, 1)
    e_ref[...] = jnp.where(col == 0, i1, i2).astype(I32)
    p_ref[...] = jnp.where(col == 0, 1.0 / den, e2 / den)


def _router(x, w_router, b_router):
    m = x.shape[0]
    tm = _row_tile(m)
    w_pad = jnp.zeros((D_MODEL, LANES), F32).at[:, :N_EXPERTS].set(w_router)
    b_pad = jnp.full((1, LANES), NEG_INF, F32).at[0, :N_EXPERTS].set(b_router)
    row = lambda i: (i, 0)
    return pl.pallas_call(
        _router_body,
        grid=(m // tm,),
        in_specs=[pl.BlockSpec((tm, D_MODEL), row), _resident((D_MODEL, LANES)), _resident((1, LANES))],
        out_specs=[pl.BlockSpec((tm, 2), row), pl.BlockSpec((tm, 2), row)],
        out_shape=[jax.ShapeDtypeStruct((m, 2), I32), jax.ShapeDtypeStruct((m, 2), F32)],
        compiler_params=_cparams(1),
        name="moe_router",
    )(x, w_pad, b_pad)


def _row_copy(src, src_row, dst, dst_row, sem):
    return pltpu.make_async_copy(src.at[pl.ds(src_row, 1)], dst.at[pl.ds(dst_row, 1)], sem)


def _gather_body(na_ref, src_ref, x_hbm, o_ref, sem, *, tm):
    i = pl.program_id(0)

    @pl.when(i < na_ref[0])
    def _():
        def issue(r, carry):
            _row_copy(x_hbm, src_ref[0, 0, r], o_ref, r, sem).start()
            return carry

        def drain(r, carry):
            _row_copy(x_hbm, 0, o_ref, 0, sem).wait()
            return carry

        lax.fori_loop(0, tm, issue, 0)
        lax.fori_loop(0, tm, drain, 0)

    @pl.when(i >= na_ref[0])
    def _():
        o_ref[...] = jnp.zeros(o_ref.shape, F32)


def _gather_rows(x, src, n_active, tm):
    n_rows = src.shape[0]
    return pl.pallas_call(
        functools.partial(_gather_body, tm=tm),
        grid_spec=pltpu.PrefetchScalarGridSpec(
            num_scalar_prefetch=1,
            grid=(n_rows // tm,),
            in_specs=[pl.BlockSpec((1, 1, tm), lambda i, na: (i, 0, 0), memory_space=pltpu.SMEM),
                      pl.BlockSpec(memory_space=pl.ANY)],
            out_specs=pl.BlockSpec((tm, D_MODEL), lambda i, na: (i, 0)),
            scratch_shapes=[pltpu.SemaphoreType.DMA(())],
        ),
        out_shape=jax.ShapeDtypeStruct((n_rows, D_MODEL), F32),
        compiler_params=_cparams(1),
        name="moe_dispatch",
    )(n_active, src.reshape(-1, 1, tm), x)


def _combine_body(p0_ref, p1_ref, y_hbm, pw_ref, x_ref, g_ref, b_ref, o_ref, buf, sem, *, tm):
    def issue(r, carry):
        _row_copy(y_hbm, p0_ref[0, 0, r], buf.at[0], r, sem).start()
        _row_copy(y_hbm, p1_ref[0, 0, r], buf.at[1], r, sem).start()
        return carry

    def drain(r, carry):
        _row_copy(y_hbm, 0, buf.at[0], 0, sem).wait()
        _row_copy(y_hbm, 0, buf.at[1], 0, sem).wait()
        return carry

    lax.fori_loop(0, tm, issue, 0)
    lax.fori_loop(0, tm, drain, 0)
    pw = pw_ref[...]
    y = pw[:, 0:1] * buf[0] + pw[:, 1:2] * buf[1]
    o_ref[...] = _layer_norm(ALPHA * x_ref[...] + y, g_ref[...], b_ref[...])


def _combine_ln(ys, pos0, pos1, pw, x, g, b):
    m = x.shape[0]
    tm = _row_tile(m)
    row = lambda i: (i, 0)
    smem = lambda: pl.BlockSpec((1, 1, tm), lambda i: (i, 0, 0), memory_space=pltpu.SMEM)
    return pl.pallas_call(
        functools.partial(_combine_body, tm=tm),
        grid=(m // tm,),
        in_specs=[smem(), smem(), pl.BlockSpec(memory_space=pl.ANY), pl.BlockSpec((tm, 2), row),
                  pl.BlockSpec((tm, D_MODEL), row), _resident((1, D_MODEL)), _resident((1, D_MODEL))],
        out_specs=pl.BlockSpec((tm, D_MODEL), row),
        out_shape=jax.ShapeDtypeStruct((m, D_MODEL), F32),
        scratch_shapes=[pltpu.VMEM((2, tm, D_MODEL), F32), pltpu.SemaphoreType.DMA(())],
        compiler_params=_cparams(1),
        name="moe_combine_ln",
    )(pos0.reshape(-1, 1, tm), pos1.reshape(-1, 1, tm), ys, pw, x, g.reshape(1, D_MODEL), b.reshape(1, D_MODEL))


def _moe(x, w_router, b_router, w_in, w_out, g, b):
    m = x.shape[0]
    tm = _row_tile(m)
    experts, probs = _router(x, w_router, b_router)
    e_flat = jnp.concatenate([experts[:, 0], experts[:, 1]])
    one_hot = (e_flat[:, None] == jnp.arange(N_EXPERTS, dtype=I32)[None, :]).astype(I32)
    csum = jnp.cumsum(one_hot, axis=0)
    rank = jnp.sum(one_hot * csum, axis=1) - 1
    counts = csum[-1]
    tiles_per_expert = (counts + tm - 1) // tm
    tile_end = jnp.cumsum(tiles_per_expert)
    start = (tile_end - tiles_per_expert) * tm
    pos = jnp.sum(one_hot * start[None, :], axis=1) + rank
    n_tiles = (2 * m) // tm + N_EXPERTS
    tile_expert = jnp.minimum(
        jnp.sum(jnp.arange(n_tiles, dtype=I32)[:, None] >= tile_end[None, :], axis=1), N_EXPERTS - 1
    ).astype(I32)
    n_active = tile_end[-1:].astype(I32)
    order = jnp.argsort(e_flat, stable=True).astype(I32)
    local = (jnp.arange(n_tiles * tm, dtype=I32).reshape(n_tiles, tm)
             - start[tile_expert][:, None])
    unpadded = (jnp.cumsum(counts) - counts)[tile_expert][:, None] + local
    valid = local < counts[tile_expert][:, None]
    src = jnp.where(valid, order[jnp.clip(unpadded, 0, 2 * m - 1)] % m, 0).reshape(-1)

    xs = _gather_rows(x, src, n_active, tm)
    ys = _ffn(xs, w_in, w_out, tile_expert, n_active, tm, name="moe_ffn")
    return _combine_ln(ys, pos[:m], pos[m:], probs, x, g, b)


def _rope_table_body(f_ref, cos_ref, sin_ref, *, past):
    pos = past + lax.broadcasted_iota(I32, cos_ref.shape, 0)
    ang = pos.astype(F32) * f_ref[...]
    lane = lax.broadcasted_iota(I32, cos_ref.shape, 1)
    cos_ref[...] = jnp.cos(ang)
    sin_ref[...] = jnp.where(lane < MLA_NOPE_DIM + MLA_ROPE_DIM // 2, -jnp.sin(ang), jnp.sin(ang))


def _rope_tables(t, past):
    half = MLA_ROPE_DIM // 2
    inv_freq = ROPE_THETA ** (-jnp.arange(half, dtype=F32) / half)
    f = jnp.zeros((1, LANES), F32)
    f = f.at[0, MLA_NOPE_DIM:MLA_NOPE_DIM + half].set(inv_freq)
    f = f.at[0, MLA_NOPE_DIM + half:MLA_QK_DIM].set(inv_freq)
    return pl.pallas_call(
        functools.partial(_rope_table_body, past=past),
        out_shape=[jax.ShapeDtypeStruct((t, LANES), F32)] * 2,
        name="rope_table",
    )(f)


def _rope_block(z, cosv, sinv):
    lane = lax.broadcasted_iota(I32, z.shape, 1)
    half = MLA_ROPE_DIM // 2
    partner = jnp.where(lane < MLA_NOPE_DIM + half,
                        pltpu.roll(z, LANES - half, 1), pltpu.roll(z, half, 1))
    return z * cosv + partner * sinv


def _mla_q_body(x_ref, wdq_ref, gq_ref, wuq_ref, cos_ref, sin_ref, o_ref):
    cq = _rms(_dot(x_ref[...].astype(BF16), wdq_ref[...]), gq_ref[...]).astype(BF16)
    cosv = cos_ref[...]
    sinv = sin_ref[...]
    width = 4 * LANES
    for c0 in range(0, MLA_HEADS * LANES, width):
        r = _dot(cq, wuq_ref[:, c0:c0 + width])
        for j in range(0, width, LANES):
            o_ref[:, c0 + j:c0 + j + LANES] = _rope_block(r[:, j:j + LANES], cosv, sinv).astype(BF16)


def _pos_tables(tables, m, t):
    tm = _row_tile(m)
    if tm <= t:
        assert t % tm == 0
        n = t // tm
        return tables, pl.BlockSpec((tm, LANES), lambda i: (i % n, 0))
    assert tm % t == 0
    return [jnp.tile(tb, (tm // t, 1)) for tb in tables], pl.BlockSpec((tm, LANES), lambda i: (0, 0))


def _mla_q(x, w_dq, g_q, w_uq_pad, tables, t):
    m = x.shape[0]
    tm = _row_tile(m)
    (cos_t, sin_t), tspec = _pos_tables(tables, m, t)
    row = lambda i: (i, 0)
    return pl.pallas_call(
        _mla_q_body,
        grid=(m // tm,),
        in_specs=[pl.BlockSpec((tm, D_MODEL), row), _resident(w_dq.shape), _resident((1, MLA_Q_RANK)),
                  _resident(w_uq_pad.shape), tspec, tspec],
        out_specs=pl.BlockSpec((tm, MLA_HEADS * LANES), row),
        out_shape=jax.ShapeDtypeStruct((m, MLA_HEADS * LANES), BF16),
        compiler_params=_cparams(1),
        name="mla_q",
    )(x, w_dq, g_q.reshape(1, MLA_Q_RANK), w_uq_pad, cos_t, sin_t)


def _mla_kv_down_body(x_ref, w_ref, g_ref, cos_ref, sin_ref, ckv_ref, kr_ref):
    kv = _dot(x_ref[...].astype(BF16), w_ref[...])
    ckv_ref[...] = _rms(kv[:, :MLA_KV_RANK], g_ref[...])
    kr_ref[...] = _rope_block(kv[:, MLA_KV_RANK:], cos_ref[...], sin_ref[...])


def _mla_kv_down(x, w_dkv_pad, g_kv, tables, t):
    m = x.shape[0]
    tm = _row_tile(m)
    (cos_t, sin_t), tspec = _pos_tables(tables, m, t)
    row = lambda i: (i, 0)
    return pl.pallas_call(
        _mla_kv_down_body,
        grid=(m // tm,),
        in_specs=[pl.BlockSpec((tm, D_MODEL), row), _resident(w_dkv_pad.shape),
                  _resident((1, MLA_KV_RANK)), tspec, tspec],
        out_specs=[pl.BlockSpec((tm, MLA_KV_RANK), row), pl.BlockSpec((tm, LANES), row)],
        out_shape=[jax.ShapeDtypeStruct((m, MLA_KV_RANK), F32), jax.ShapeDtypeStruct((m, LANES), F32)],
        compiler_params=_cparams(1),
        name="mla_kv_down",
    )(x, w_dkv_pad, g_kv.reshape(1, MLA_KV_RANK), cos_t, sin_t)


def _mla_kv_up_body(c_ref, kr_ref, wuk_ref, wuv_ref, k_ref, v_ref):
    cb = c_ref[...].astype(BF16)
    kr = kr_ref[...]
    width = 4 * LANES
    for c0 in range(0, MLA_HEADS * LANES, width):
        r = _dot(cb, wuk_ref[:, c0:c0 + width])
        for j in range(0, width, LANES):
            k_ref[:, c0 + j:c0 + j + LANES] = (r[:, j:j + LANES] + kr).astype(BF16)
    for c0 in range(0, MLA_HEADS * MLA_V_DIM, width):
        v_ref[:, c0:c0 + width] = _dot(cb, wuv_ref[:, c0:c0 + width]).astype(BF16)


def _mla_kv_up(ckv, kr_pad, w_uk_pad, w_uv):
    m = ckv.shape[0]
    tm = ROW_TILE if m % ROW_TILE == 0 else (LANES // 8 if m % LANES else LANES)
    assert m % tm == 0
    row = lambda i: (i, 0)
    return pl.pallas_call(
        _mla_kv_up_body,
        grid=(m // tm,),
        in_specs=[pl.BlockSpec((tm, MLA_KV_RANK), row), pl.BlockSpec((tm, LANES), row),
                  _resident(w_uk_pad.shape), _resident(w_uv.shape)],
        out_specs=[pl.BlockSpec((tm, MLA_HEADS * LANES), row), pl.BlockSpec((tm, MLA_HEADS * MLA_V_DIM), row)],
        out_shape=[jax.ShapeDtypeStruct((m, MLA_HEADS * LANES), BF16),
                   jax.ShapeDtypeStruct((m, MLA_HEADS * MLA_V_DIM), BF16)],
        compiler_params=_cparams(1),
        name="mla_kv_up",
    )(ckv, kr_pad, w_uk_pad, w_uv)


def _pool_body(prev_ref, halo_ref, x_ref, wp_ref, sc_ref, g_ref, b_ref, o_ref, ext, *, tm, pos0):
    i = pl.program_id(1)
    x = x_ref[0]
    ext[HALO:, :] = x

    @pl.when(i == 0)
    def _():
        ext[:HALO, :] = prev_ref[0]

    @pl.when(i > 0)
    def _():
        ext[:HALO, :] = halo_ref[0]

    pos = pos0 + i * tm + lax.broadcasted_iota(I32, (tm, 1), 0)
    for gi, w in enumerate(POOL_WINDOWS):
        ch = slice(gi * POOL_GROUP, (gi + 1) * POOL_GROUP)
        win = x[:, ch]
        for j in range(1, w):
            win = win + ext[HALO - j:HALO - j + tm, ch]
        count = jnp.minimum(pos + 1, w).astype(F32)
        mixed = (win / count - x[:, ch]).astype(BF16)
        o_ref[0, :, ch] = _dot(mixed, wp_ref[gi])
    y = o_ref[0] * sc_ref[...]
    o_ref[0] = _layer_norm(ALPHA * x + y, g_ref[...], b_ref[...])


def _pool_ln(x, prev_rows, pos0, w_pool, scale, g, b):
    bsz, t, _ = x.shape
    tm = _row_tile(t)
    prev = jnp.concatenate([jnp.zeros((bsz, HALO - POOL_STATE, D_MODEL), F32), prev_rows], axis=1)
    per = tm // HALO
    tile = lambda b_, i: (b_, i, 0)
    vec = pl.BlockSpec((1, D_MODEL), lambda b_, i: (0, 0))
    return pl.pallas_call(
        functools.partial(_pool_body, tm=tm, pos0=pos0),
        grid=(bsz, t // tm),
        in_specs=[pl.BlockSpec((1, HALO, D_MODEL), lambda b_, i: (b_, 0, 0)),
                  pl.BlockSpec((1, HALO, D_MODEL), lambda b_, i: (b_, jnp.maximum(i * per - 1, 0), 0)),
                  pl.BlockSpec((1, tm, D_MODEL), tile),
                  pl.BlockSpec(w_pool.shape, lambda b_, i: (0, 0, 0)), vec, vec, vec],
        out_specs=pl.BlockSpec((1, tm, D_MODEL), tile),
        out_shape=jax.ShapeDtypeStruct((bsz, t, D_MODEL), F32),
        scratch_shapes=[pltpu.VMEM((tm + HALO, D_MODEL), F32)],
        compiler_params=_cparams(2),
        name="pool_ln",
    )(prev, x, x, w_pool, scale.reshape(1, D_MODEL), g.reshape(1, D_MODEL), b.reshape(1, D_MODEL))


def _gla_proj_body(x_ref, win_ref, wa1_ref, wa2_ref, ba_ref, q_ref, k_ref, v_ref, gt_ref, la_ref):
    xb = x_ref[...].astype(BF16)
    width = 2 * MXU_DIM
    for o_ref, c0, c1 in ((q_ref, 0, GLA_KEY), (k_ref, GLA_KEY, 2 * GLA_KEY),
                          (v_ref, 2 * GLA_KEY, 2 * GLA_KEY + GLA_VAL),
                          (gt_ref, 2 * GLA_KEY + GLA_VAL, 2 * GLA_KEY + 2 * GLA_VAL)):
        for j0 in range(c0, c1, width):
            o_ref[:, j0 - c0:j0 - c0 + width] = _dot(xb, win_ref[:, j0:j0 + width]).astype(o_ref.dtype)
    low = _dot(xb, wa1_ref[...]).astype(BF16)
    z = _dot(low, wa2_ref[...]) + ba_ref[...]
    la_ref[...] = (jnp.minimum(z, 0.0) - jnp.log(1.0 + jnp.exp(-jnp.abs(z)))) / GLA_TAU


def _gla_proj(x, w_in, w_a1_pad, w_a2_pad, b_a):
    m = x.shape[0]
    tm = _row_tile(m)
    row = lambda i: (i, 0)
    widths = (GLA_KEY, GLA_KEY, GLA_VAL, GLA_VAL, GLA_KEY)
    dtypes = (F32, F32, BF16, F32, F32)
    return pl.pallas_call(
        _gla_proj_body,
        grid=(m // tm,),
        in_specs=[pl.BlockSpec((tm, D_MODEL), row), _resident(w_in.shape), _resident(w_a1_pad.shape),
                  _resident(w_a2_pad.shape), _resident((1, GLA_KEY))],
        out_specs=[pl.BlockSpec((tm, w), row) for w in widths],
        out_shape=[jax.ShapeDtypeStruct((m, w), d) for w, d in zip(widths, dtypes)],
        compiler_params=_cparams(1),
        name="gla_proj",
    )(x, w_in, w_a1_pad, w_a2_pad, b_a.reshape(1, GLA_KEY))


def _cumsum_rows(z):
    n = z.shape[0]
    row = lax.broadcasted_iota(I32, z.shape, 0)
    shift = 1
    while shift < n:
        z = z + jnp.where(row >= shift, pltpu.roll(z, shift, 0), 0.0)
        shift *= 2
    return z


def _gla_body(q_ref, k_ref, la_ref, v_ref, gt_ref, s0_ref, hg_ref, o_ref, s_ref, *, blk, n_blk):
    s_ref[0, 0] = s0_ref[0, 0]
    row = lax.broadcasted_iota(I32, (blk, blk), 0)
    col = lax.broadcasted_iota(I32, (blk, blk), 1)
    causal = row >= col

    def block(n, carry):
        r0 = pl.multiple_of(n * blk, blk)
        rows = pl.ds(r0, blk)
        cum = _cumsum_rows(la_ref[0, rows, :])
        cum_last = cum[blk - 1:blk, :]
        q = q_ref[0, rows, :] * (GLA_DK ** -0.5)
        k = k_ref[0, rows, :]
        v = v_ref[0, rows, :]
        q_dec = (q * jnp.exp(cum)).astype(BF16)
        k_inv = (k * jnp.exp(-cum)).astype(BF16)
        k_end = (k * jnp.exp(cum_last - cum)).astype(BF16)
        scores = jnp.where(causal, _dot_nt(q_dec, k_inv), 0.0).astype(BF16)
        state = s_ref[0, 0]
        o = _dot(scores, v) + _dot_nt(q_dec, state.astype(BF16))
        s_ref[0, 0] = jnp.exp(cum_last) * state + _dot_tn(v, k_end)
        gate = gt_ref[0, rows, :]
        o_ref[0, rows, :] = (_rms(o, hg_ref[...]) * jax.nn.silu(gate)).astype(BF16)
        return carry

    lax.fori_loop(0, n_blk, block, 0)


def _gla(q, k, la, v, gate, s0_t, head_g):
    bsz, t, _ = q.shape
    blk = GLA_BLOCK if t % GLA_BLOCK == 0 else t
    kspec = pl.BlockSpec((1, t, GLA_DK), lambda b_, h: (b_, 0, h))
    vspec = pl.BlockSpec((1, t, GLA_DV), lambda b_, h: (b_, 0, h))
    sspec = pl.BlockSpec((1, 1, GLA_DV, GLA_DK), lambda b_, h: (b_, h, 0, 0))
    return pl.pallas_call(
        functools.partial(_gla_body, blk=blk, n_blk=t // blk),
        grid=(bsz, GLA_HEADS),
        in_specs=[kspec, kspec, kspec, vspec, vspec, sspec, pl.BlockSpec((1, GLA_DV), lambda b_, h: (0, 0))],
        out_specs=[vspec, sspec],
        out_shape=[jax.ShapeDtypeStruct((bsz, t, GLA_VAL), BF16),
                   jax.ShapeDtypeStruct((bsz, GLA_HEADS, GLA_DV, GLA_DK), F32)],
        compiler_params=_cparams(2),
        name="gla",
    )(q, k, la, v, gate, s0_t, head_g.reshape(1, GLA_DV))


def _pad_head_cols(w, heads, width):
    k = w.shape[0]
    w = w.reshape(k, heads, width)
    return jnp.pad(w, ((0, 0), (0, 0), (0, LANES - width))).reshape(k, heads * LANES)


def kernel(x_prompt, x_sample, cache_a_k, cache_a_v, cache_b_ckv, cache_b_krope, state_c_rows, state_d_gla, a_w_qkv, a_lam_q1, a_lam_k1, a_lam_q2, a_lam_k2, a_subln_g, a_w_o, b_w_dq, b_q_norm_g, b_w_uq, b_w_dkv, b_kv_norm_g, b_w_uk, b_w_uv, b_w_o, c_w_pool, c_scale, d_w_in, d_w_a1, d_w_a2, d_b_a, d_head_g, d_w_o, ln_g, ln_b, ffn_w_in, ffn_w_out, moe_w_router, moe_b_router, moe_w_in, moe_w_out):
    bf = lambda w: w.astype(BF16)
    d = D_MODEL
    past = cache_a_k.shape[1]
    streams = [(x_prompt.shape[0], x_prompt.shape[1], 0), (x_sample.shape[0], x_sample.shape[1], past)]
    xs = [x_prompt.reshape(-1, d), x_sample.reshape(-1, d)]

    ffn_in, ffn_out = bf(ffn_w_in), bf(ffn_w_out)
    moe_in, moe_out = bf(moe_w_in), bf(moe_w_out)

    w_qkv, w_o = bf(a_w_qkv), bf(a_w_o)
    lam_vecs = jnp.stack([a_lam_q1, a_lam_k1, a_lam_q2, a_lam_k2]).astype(F32)
    lambda_init = 0.8 - 0.6 * math.exp(-0.3 * 0)
    caches = [(None, None), (cache_a_k, cache_a_v)]
    a_k, a_v = [], []
    for s, ((bsz, t, p), (ck, cv)) in enumerate(zip(streams, caches)):
        qb, kf, kb, vf, vb = _proj(
            xs[s], w_qkv,
            [(0, d, [(BF16, DA_HEAD_DIM ** -0.5)]), (d, 2 * d, [(F32, 1.0), (BF16, 1.0)]),
             (2 * d, 3 * d, [(F32, 1.0), (BF16, 1.0)])], name="a_qkv")
        kb, vb = kb.reshape(bsz, t, d), vb.reshape(bsz, t, d)
        if p:
            kb = jnp.concatenate([bf(ck.reshape(bsz, p, d)), kb], axis=1)
            vb = jnp.concatenate([bf(cv.reshape(bsz, p, d)), vb], axis=1)
        o = _attention("diff", qb.reshape(bsz, t, d), kb, vb, lam_vecs=lam_vecs, subln_g=a_subln_g,
                       lambda_init=lambda_init, name="a_attn")
        xs[s] = _oproj_ln(o.reshape(-1, d), w_o, xs[s], ln_g[0, 0], ln_b[0, 0], name="a_out_ln")
        xs[s] = _dense_ffn(xs[s], ffn_in, ffn_out, 0, ln_g[0, 1], ln_b[0, 1], name="ffn0")
        a_k.append(kf.reshape(bsz, t, DA_HEADS, 2, DA_HEAD_DIM))
        a_v.append(vf.reshape(bsz, t, DA_HEADS, 2 * DA_HEAD_DIM))

    w_dq, w_o = bf(b_w_dq), bf(b_w_o)
    w_uq_pad = bf(_pad_head_cols(b_w_uq, MLA_HEADS, MLA_QK_DIM))
    w_uk_pad = bf(_pad_head_cols(b_w_uk, MLA_HEADS, MLA_NOPE_DIM))
    w_uv = bf(b_w_uv)
    w_dkv_pad = bf(jnp.concatenate(
        [b_w_dkv[:, :MLA_KV_RANK], jnp.zeros((d, MLA_NOPE_DIM), F32), b_w_dkv[:, MLA_KV_RANK:],
         jnp.zeros((d, LANES - MLA_QK_DIM), F32)], axis=1))
    caches = [(None, None), (cache_b_ckv, cache_b_krope)]
    b_ckv, b_kr = [], []
    for s, ((bsz, t, p), (cc, ckr)) in enumerate(zip(streams, caches)):
        tables = _rope_tables(t, p)
        q = _mla_q(xs[s], w_dq, b_q_norm_g, w_uq_pad, tables, t)
        ckv, kr_pad = _mla_kv_down(xs[s], w_dkv_pad, b_kv_norm_g, tables, t)
        ckv_all, kr_all = ckv, kr_pad
        if p:
            ckr_pad = jnp.pad(ckr, ((0, 0), (0, 0), (MLA_NOPE_DIM, LANES - MLA_QK_DIM)))
            ckv_all = jnp.concatenate([cc, ckv.reshape(bsz, t, -1)], axis=1).reshape(-1, MLA_KV_RANK)
            kr_all = jnp.concatenate([ckr_pad, kr_pad.reshape(bsz, t, -1)], axis=1).reshape(-1, LANES)
        k_pad, v = _mla_kv_up(ckv_all, kr_all, w_uk_pad, w_uv)
        o = _attention("pair", q.reshape(bsz, t, -1), k_pad.reshape(bsz, p + t, -1),
                       v.reshape(bsz, p + t, -1), scale=MLA_QK_DIM ** -0.5, name="b_attn")
        xs[s] = _oproj_ln(o.reshape(-1, d), w_o, xs[s], ln_g[1, 0], ln_b[1, 0], name="b_out_ln")
        xs[s] = _moe(xs[s], moe_w_router[0], moe_b_router[0], moe_in[0], moe_out[0], ln_g[1, 1], ln_b[1, 1])
        b_ckv.append(ckv.reshape(bsz, t, MLA_KV_RANK))
        b_kr.append(kr_pad[:, MLA_NOPE_DIM:MLA_QK_DIM].reshape(bsz, t, MLA_ROPE_DIM))

    w_pool = bf(c_w_pool)
    prevs = [jnp.zeros((streams[0][0], POOL_STATE, d), F32), state_c_rows]
    c_rows = []
    for s, ((bsz, t, p), prev) in enumerate(zip(streams, prevs)):
        x3 = xs[s].reshape(bsz, t, d)
        c_rows.append(jnp.concatenate([prev, x3], axis=1)[:, -POOL_STATE:])
        xs[s] = _pool_ln(x3, prev, p, w_pool, c_scale, ln_g[2, 0], ln_b[2, 0]).reshape(-1, d)
        xs[s] = _dense_ffn(xs[s], ffn_in, ffn_out, 1, ln_g[2, 1], ln_b[2, 1], name="ffn1")

    w_in, w_o = bf(d_w_in), bf(d_w_o)
    w_a1_pad = bf(jnp.pad(d_w_a1, ((0, 0), (0, LANES - GLA_GATE_RANK))))
    w_a2_pad = bf(jnp.pad(d_w_a2, ((0, LANES - GLA_GATE_RANK), (0, 0))))
    states = [jnp.zeros((streams[0][0], GLA_HEADS, GLA_DK, GLA_DV), F32), state_d_gla]
    d_gla = []
    for s, ((bsz, t, p), s0) in enumerate(zip(streams, states)):
        q, k, v, gate, la = _gla_proj(xs[s], w_in, w_a1_pad, w_a2_pad, d_b_a)
        shp = lambda z: z.reshape(bsz, t, -1)
        o, s_t = _gla(shp(q), shp(k), shp(la), shp(v), shp(gate), jnp.swapaxes(s0, 2, 3), d_head_g)
        xs[s] = _oproj_ln(o.reshape(-1, d), w_o, xs[s], ln_g[3, 0], ln_b[3, 0], name="d_out_ln")
        xs[s] = _moe(xs[s], moe_w_router[1], moe_b_router[1], moe_in[1], moe_out[1], ln_g[3, 1], ln_b[3, 1])
        d_gla.append(jnp.swapaxes(s_t, 2, 3))

    y = [xs[s].reshape(bsz, t, d) for s, (bsz, t, _) in enumerate(streams)]
    return (y[0], y[1], a_k[0], a_v[0], a_k[1], a_v[1], b_ckv[0], b_kr[0], b_ckv[1], b_kr[1],
            c_rows[0], c_rows[1], d_gla[0], d_gla[1])
```

```python
import functools
import math

import jax
import jax.numpy as jnp
from jax import lax
from jax.experimental import pallas as pl
from jax.experimental.pallas import tpu as pltpu

F32 = jnp.float32
BF16 = jnp.bfloat16
I32 = jnp.int32

D_MODEL = 1024
DEPTH = 4
CHUNK = 64
ALPHA = (2 * DEPTH) ** 0.25
LN_EPS = 1e-5
RMS_EPS = 1e-6
NEG_INF = -1e30
LOG2E = math.log2(math.e)

DA_HEADS = 8
DA_HEAD_DIM = 64
MLA_HEADS = 16
MLA_NOPE_DIM = 64
MLA_ROPE_DIM = 32
MLA_QK_DIM = MLA_NOPE_DIM + MLA_ROPE_DIM
MLA_V_DIM = 64
MLA_Q_RANK = 384
MLA_KV_RANK = 256
ROPE_THETA = 10000.0
POOL_WINDOWS = (2, 4, 8, 16)
POOL_GROUP = D_MODEL // len(POOL_WINDOWS)
POOL_STATE = max(POOL_WINDOWS) - 1
GLA_HEADS = 4
GLA_DK = 128
GLA_DV = 256
GLA_KEY = GLA_HEADS * GLA_DK
GLA_VAL = GLA_HEADS * GLA_DV
GLA_GATE_RANK = 16
GLA_TAU = 16.0
GLA_BLOCK = CHUNK
N_EXPERTS = 8

LANES = 128
MXU_DIM = 256
VMEM_LIMIT_BYTES = 56 * 1024 * 1024
ROW_TILE = 512
HALO = 16
DMA_UNROLL = 8


def _cparams(n_axes):
    return pltpu.CompilerParams(
        dimension_semantics=("arbitrary",) * n_axes, vmem_limit_bytes=VMEM_LIMIT_BYTES)


def _row_tile(m):
    return ROW_TILE if m % ROW_TILE == 0 else m


def _resident(shape):
    nd = len(shape)
    return pl.BlockSpec(shape, lambda *_: (0,) * nd)


def _layer_norm(z, g, b):
    mu = jnp.mean(z, axis=-1, keepdims=True)
    zc = z - mu
    var = jnp.mean(zc * zc, axis=-1, keepdims=True)
    return zc * lax.rsqrt(var + LN_EPS) * g + b


def _rms(z, g):
    return z * lax.rsqrt(jnp.mean(z * z, axis=-1, keepdims=True) + RMS_EPS) * g


def _dot(a, b):
    return jnp.dot(a, b, preferred_element_type=F32)


def _dot_nt(a, b):
    return lax.dot_general(a, b, (((1,), (1,)), ((), ())), preferred_element_type=F32)


def _dot_tn(a, b):
    return lax.dot_general(a, b, (((0,), (0,)), ((), ())), preferred_element_type=F32)


def _proj_body(x_ref, w_ref, *o_refs, groups, chunk):
    xb = x_ref[...].astype(BF16)
    o_refs = iter(o_refs)
    for c0, c1, sinks in groups:
        refs = [next(o_refs) for _ in sinks]
        for j0 in range(c0, c1, chunk):
            j1 = min(j0 + chunk, c1)
            r = _dot(xb, w_ref[:, j0:j1])
            for o_ref, (_, sc) in zip(refs, sinks):
                o_ref[:, j0 - c0:j1 - c0] = (r if sc == 1.0 else r * sc).astype(o_ref.dtype)


def _proj(x, w, groups, name):
    m, k = x.shape
    tm = _row_tile(m)
    outs = [(c1 - c0, dt) for c0, c1, sinks in groups for dt, _ in sinks]
    return pl.pallas_call(
        functools.partial(_proj_body, groups=groups, chunk=2 * MXU_DIM),
        grid=(m // tm,),
        in_specs=[pl.BlockSpec((tm, k), lambda i: (i, 0)), _resident(w.shape)],
        out_specs=[pl.BlockSpec((tm, wd), lambda i: (i, 0)) for wd, _ in outs],
        out_shape=[jax.ShapeDtypeStruct((m, wd), dt) for wd, dt in outs],
        compiler_params=_cparams(1),
        name=name,
    )(x, w)


def _oproj_ln_body(a_ref, w_ref, x_ref, g_ref, b_ref, o_ref):
    y = _dot(a_ref[...], w_ref[...])
    o_ref[...] = _layer_norm(ALPHA * x_ref[...] + y, g_ref[...], b_ref[...])


def _oproj_ln(a, w, x, g, b, name):
    m, kd = a.shape
    tm = _row_tile(m)
    row = lambda i: (i, 0)
    return pl.pallas_call(
        _oproj_ln_body,
        grid=(m // tm,),
        in_specs=[pl.BlockSpec((tm, kd), row), _resident(w.shape), pl.BlockSpec((tm, D_MODEL), row),
                  _resident((1, D_MODEL)), _resident((1, D_MODEL))],
        out_specs=pl.BlockSpec((tm, D_MODEL), row),
        out_shape=jax.ShapeDtypeStruct((m, D_MODEL), F32),
        compiler_params=_cparams(1),
        name=name,
    )(a, w, x, g.reshape(1, D_MODEL), b.reshape(1, D_MODEL))


def _attn_plan(tq_total, tk_valid):
    past = tk_valid - tq_total
    if past == 0:
        t = min(ROW_TILE, tq_total)
        assert tq_total % t == 0 and t % LANES == 0
        return t, t, tq_total // t, None, t, tk_valid
    tq = tq_total
    first_chunk = past // CHUNK
    all_visible = min((first_chunk + 1) * CHUNK, tk_valid)
    tk = ROW_TILE
    n_full = all_visible // tk
    last_chunk = (past + tq - 1) // CHUNK
    end = min((last_chunk + 1) * CHUNK, tk_valid)
    tail = -(-(end - n_full * tk) // LANES) * LANES
    return tq, tk, 1, n_full, tail, max(n_full * tk + tail, tk_valid)


def _attn_body(*refs, mode, tq, tk, n_full_static, tail, past, tk_valid, scale, lambda_init):
    if mode == "diff":
        lam_ref, q_ref, k_ref, v_ref, g_ref, o_ref, m_sc, l_sc, acc_sc = refs
    else:
        q_ref, k_ref, v_ref, o_ref, m_sc, l_sc, acc_sc = refs
    qi = pl.program_id(2)
    n_full = qi if n_full_static is None else n_full_static
    rows = 2 * tq
    exp = jnp.exp if mode == "diff" else jnp.exp2
    lane = lax.broadcasted_iota(I32, (rows, LANES), 1)
    lo = lax.broadcasted_iota(I32, (tq, LANES), 1) < LANES // 2
    lo_row = lax.broadcasted_iota(I32, (1, LANES), 1) < LANES // 2

    q = q_ref[0]
    if mode == "diff":
        zero = jnp.zeros_like(q)
        q_both = jnp.concatenate([jnp.where(lo, q, zero), jnp.where(lo, zero, q)], axis=0)
    else:
        q0 = q[:, :LANES]
        q1 = q[:, LANES:]

    m_sc[...] = jnp.full(m_sc.shape, NEG_INF, F32)
    l_sc[...] = jnp.zeros(l_sc.shape, F32)
    acc_sc[...] = jnp.zeros(acc_sc.shape, F32)

    def tile(off, length, masked):
        k = k_ref[0, pl.ds(off, length), :]
        v = v_ref[0, pl.ds(off, length), :]
        if mode == "diff":
            s = _dot_nt(q_both, k)
        else:
            s = jnp.concatenate([_dot_nt(q0, k[:, :LANES]), _dot_nt(q1, k[:, LANES:])], axis=0)
            s = s * (scale * LOG2E)
        blocks = [s[:, c:c + LANES] for c in range(0, length, LANES)]
        if masked:
            row = lax.broadcasted_iota(I32, (rows, LANES), 0)
            q_pos = past + qi * tq + jnp.where(row >= tq, row - tq, row)
            limit = jnp.minimum((q_pos // CHUNK + 1) * CHUNK, tk_valid) - off
            blocks = [jnp.where(lane + c * LANES < limit, blk, NEG_INF) for c, blk in enumerate(blocks)]
        m_old = m_sc[...]
        m_new = jnp.maximum(
            m_old, jnp.max(functools.reduce(jnp.maximum, blocks), axis=-1, keepdims=True))
        alpha = exp(m_old - m_new)
        probs = [exp(blk - m_new) for blk in blocks]
        l_sc[...] = alpha * l_sc[...] + functools.reduce(jnp.add, probs)
        m_sc[...] = m_new
        pb = jnp.concatenate([p.astype(BF16) for p in probs], axis=1)
        if mode == "diff":
            acc_sc[...] = alpha * acc_sc[...] + _dot(pb, v)
        else:
            zero = jnp.zeros_like(v)
            v0 = jnp.where(lo_row, v, zero)
            v1 = jnp.where(lo_row, zero, v)
            alpha_l = jnp.where(lo, alpha[:tq], alpha[tq:])
            acc_sc[...] = alpha_l * acc_sc[...] + _dot(pb[:tq], v0) + _dot(pb[tq:], v1)

    def full_tile(j, carry):
        tile(pl.multiple_of(j * tk, tk), tk, False)
        return carry

    lax.fori_loop(0, n_full, full_tile, 0)
    tile(n_full * tk if n_full_static is not None else pl.multiple_of(n_full * tk, tk), tail, True)

    inv_l = 1.0 / jnp.sum(l_sc[...], axis=-1, keepdims=True)
    if mode == "diff":
        lam_v = lam_ref[...]
        lam = (jnp.exp(jnp.sum(lam_v[0:1] * lam_v[1:2], keepdims=True))
               - jnp.exp(jnp.sum(lam_v[2:3] * lam_v[3:4], keepdims=True)) + lambda_init)
        acc = acc_sc[...]
        o = acc[:tq] * inv_l[:tq] - lam * (acc[tq:] * inv_l[tq:])
        o = _rms(o, g_ref[...]) * (1.0 - lambda_init)
    else:
        o = acc_sc[...] * jnp.where(lo, inv_l[:tq], inv_l[tq:])
    o_ref[0] = o.astype(o_ref.dtype)


def _attention(mode, q, k, v, *, lam_vecs=None, subln_g=None, lambda_init=0.0, scale=1.0, name):
    bsz, tq_total, _ = q.shape
    tk_valid = k.shape[1]
    qw = LANES if mode == "diff" else 2 * LANES
    heads = q.shape[2] // qw
    tq, tk, nq, n_full, tail, tk_pad = _attn_plan(tq_total, tk_valid)
    if tk_pad > tk_valid:
        k = jnp.pad(k, ((0, 0), (0, tk_pad - tk_valid), (0, 0)))
        v = jnp.pad(v, ((0, 0), (0, tk_pad - tk_valid), (0, 0)))
    body = functools.partial(
        _attn_body, mode=mode, tq=tq, tk=tk, n_full_static=n_full, tail=tail,
        past=tk_valid - tq_total, tk_valid=tk_valid, scale=scale, lambda_init=lambda_init)
    in_specs = [
        pl.BlockSpec((1, tq, qw), lambda b, h, i: (b, i, h)),
        pl.BlockSpec((1, tk_pad, qw), lambda b, h, i: (b, 0, h)),
        pl.BlockSpec((1, tk_pad, LANES), lambda b, h, i: (b, 0, h)),
    ]
    args = [q, k, v]
    if mode == "diff":
        in_specs = [_resident((4, DA_HEAD_DIM))] + in_specs + [_resident((1, LANES))]
        args = [lam_vecs] + args + [subln_g.reshape(1, LANES)]
    return pl.pallas_call(
        body,
        grid=(bsz, heads, nq),
        in_specs=in_specs,
        out_specs=pl.BlockSpec((1, tq, LANES), lambda b, h, i: (b, i, h)),
        out_shape=jax.ShapeDtypeStruct((bsz, tq_total, heads * LANES), BF16),
        scratch_shapes=[pltpu.VMEM((2 * tq, LANES), F32), pltpu.VMEM((2 * tq, LANES), F32),
                        pltpu.VMEM((2 * tq if mode == "diff" else tq, LANES), F32)],
        compiler_params=_cparams(3),
        name=name,
    )(*args)


def _ffn_body(te_ref, na_ref, x_ref, win_ref, wout_ref, *rest, dff, fused_ln):
    if fused_ln:
        g_ref, b_ref, o_ref = rest
    else:
        (o_ref,) = rest
    i = pl.program_id(0)

    @pl.when(i < na_ref[0])
    def _():
        x = x_ref[...]
        xb = x.astype(BF16)
        acc = jnp.zeros((x.shape[0], D_MODEL), F32)
        for c0 in range(0, dff, MXU_DIM):
            c1 = min(c0 + MXU_DIM, dff)
            gate = _dot(xb, win_ref[0, :, c0:c1])
            up = _dot(xb, win_ref[0, :, dff + c0:dff + c1])
            act = (jax.nn.silu(gate) * up).astype(BF16)
            acc = acc + _dot(act, wout_ref[0, c0:c1, :])
        if fused_ln:
            acc = _layer_norm(ALPHA * x + acc, g_ref[...], b_ref[...])
        o_ref[...] = acc

    @pl.when(i >= na_ref[0])
    def _():
        o_ref[...] = jnp.zeros(o_ref.shape, F32)


def _ffn(x, w_in, w_out, tile_expert, n_active, tm, ln=None, name="ffn"):
    m = x.shape[0]
    dff = w_out.shape[1]
    row_spec = pl.BlockSpec((tm, D_MODEL), lambda i, te, na: (i, 0))
    in_specs = [
        row_spec,
        pl.BlockSpec((1, D_MODEL, 2 * dff), lambda i, te, na: (te[i], 0, 0)),
        pl.BlockSpec((1, dff, D_MODEL), lambda i, te, na: (te[i], 0, 0)),
    ]
    args = [x, w_in, w_out]
    if ln is not None:
        in_specs += [pl.BlockSpec((1, D_MODEL), lambda i, te, na: (0, 0))] * 2
        args += [ln[0].reshape(1, D_MODEL), ln[1].reshape(1, D_MODEL)]
    return pl.pallas_call(
        functools.partial(_ffn_body, dff=dff, fused_ln=ln is not None),
        grid_spec=pltpu.PrefetchScalarGridSpec(
            num_scalar_prefetch=2,
            grid=(m // tm,),
            in_specs=in_specs,
            out_specs=row_spec,
        ),
        out_shape=jax.ShapeDtypeStruct(x.shape, F32),
        compiler_params=_cparams(1),
        name=name,
    )(tile_expert, n_active, *args)


def _dense_ffn(x, w_in, w_out, layer, g, b, name):
    tm = _row_tile(x.shape[0])
    n_tiles = x.shape[0] // tm
    te = jnp.full((n_tiles,), layer, I32)
    return _ffn(x, w_in, w_out, te, jnp.full((1,), n_tiles, I32), tm, ln=(g, b), name=name)


def _router_body(x_ref, w_ref, b_ref, e_ref, p_ref):
    logits = jnp.dot(x_ref[...], w_ref[...], preferred_element_type=F32,
                     precision=lax.Precision.HIGHEST) + b_ref[...]
    lane = lax.broadcasted_iota(I32, logits.shape, 1).astype(F32)
    v1 = jnp.max(logits, axis=-1, keepdims=True)
    i1 = jnp.min(jnp.where(logits == v1, lane, float(LANES)), axis=-1, keepdims=True)
    rest = jnp.where(lane == i1, NEG_INF, logits)
    v2 = jnp.max(rest, axis=-1, keepdims=True)
    i2 = jnp.min(jnp.where(rest == v2, lane, float(LANES)), axis=-1, keepdims=True)
    e2 = jnp.exp(v2 - v1)
    den = 1.0 + e2
    col = lax.broadcasted_iota(I32, e_ref.shape, 1)
    e_ref[...] = jnp.where(col == 0, i1, i2).astype(I32)
    p_ref[...] = jnp.where(col == 0, 1.0 / den, e2 / den)


def _router(x, w_router, b_router):
    m = x.shape[0]
    tm = _row_tile(m)
    w_pad = jnp.zeros((D_MODEL, LANES), F32).at[:, :N_EXPERTS].set(w_router)
    b_pad = jnp.full((1, LANES), NEG_INF, F32).at[0, :N_EXPERTS].set(b_router)
    row = lambda i: (i, 0)
    return pl.pallas_call(
        _router_body,
        grid=(m // tm,),
        in_specs=[pl.BlockSpec((tm, D_MODEL), row), _resident((D_MODEL, LANES)), _resident((1, LANES))],
        out_specs=[pl.BlockSpec((tm, 2), row), pl.BlockSpec((tm, 2), row)],
        out_shape=[jax.ShapeDtypeStruct((m, 2), I32), jax.ShapeDtypeStruct((m, 2), F32)],
        compiler_params=_cparams(1),
        name="moe_router",
    )(x, w_pad, b_pad)


def _row_copy(src, src_row, dst, dst_row, sem):
    return pltpu.make_async_copy(src.at[pl.ds(src_row, 1)], dst.at[pl.ds(dst_row, 1)], sem)


def _start_rows(src_hbm, idx_ref, dst, sem, tm):
    def issue(j, carry):
        for u in range(DMA_UNROLL):
            r = j * DMA_UNROLL + u
            _row_copy(src_hbm, idx_ref[0, 0, r], dst, r, sem).start(priority=u % 2)
        return carry

    lax.fori_loop(0, tm // DMA_UNROLL, issue, 0)


def _wait_rows(src_hbm, dst, sem, tm):
    def drain(j, carry):
        for _ in range(DMA_UNROLL):
            _row_copy(src_hbm, 0, dst, 0, sem).wait()
        return carry

    lax.fori_loop(0, tm // DMA_UNROLL, drain, 0)


def _gather_body(na_ref, src_ref, nxt_ref, x_hbm, o_ref, buf, sem, *, tm):
    i = pl.program_id(0)
    slot = i % 2

    @pl.when(i == 0)
    def _():
        _start_rows(x_hbm, src_ref, buf.at[0], sem.at[0], tm)

    @pl.when(i + 1 < na_ref[0])
    def _():
        _start_rows(x_hbm, nxt_ref, buf.at[1 - slot], sem.at[1 - slot], tm)

    @pl.when(i < na_ref[0])
    def _():
        _wait_rows(x_hbm, buf.at[slot], sem.at[slot], tm)
        o_ref[...] = buf[slot]

    @pl.when(i >= na_ref[0])
    def _():
        o_ref[...] = jnp.zeros(o_ref.shape, F32)


def _gather_rows(x, src, n_active, tm):
    n_rows = src.shape[0]
    n_tiles = n_rows // tm
    src3 = src.reshape(n_tiles, 1, tm)
    tile = (tm, D_MODEL)
    return pl.pallas_call(
        functools.partial(_gather_body, tm=tm),
        grid_spec=pltpu.PrefetchScalarGridSpec(
            num_scalar_prefetch=1,
            grid=(n_tiles,),
            in_specs=[pl.BlockSpec((1, 1, tm), lambda i, na: (i, 0, 0), memory_space=pltpu.SMEM),
                      pl.BlockSpec((1, 1, tm), lambda i, na: (jnp.minimum(i + 1, n_tiles - 1), 0, 0),
                                   memory_space=pltpu.SMEM),
                      pl.BlockSpec(memory_space=pl.ANY)],
            out_specs=pl.BlockSpec(tile, lambda i, na: (i, 0)),
            scratch_shapes=[pltpu.VMEM((2,) + tile, F32), pltpu.SemaphoreType.DMA((2,))],
        ),
        out_shape=jax.ShapeDtypeStruct((n_rows, D_MODEL), F32),
        compiler_params=_cparams(1),
        name="moe_dispatch",
    )(n_active, src3, src3, x)


def _combine_body(p0_ref, p1_ref, n0_ref, n1_ref, y_hbm, pw_ref, x_ref, g_ref, b_ref, o_ref, buf, sem,
                  *, tm):
    i = pl.program_id(0)
    slot = i % 2

    def start(q0_ref, q1_ref, s):
        _start_rows(y_hbm, q0_ref, buf.at[s, 0], sem.at[s], tm)
        _start_rows(y_hbm, q1_ref, buf.at[s, 1], sem.at[s], tm)

    @pl.when(i == 0)
    def _():
        start(p0_ref, p1_ref, 0)

    @pl.when(i + 1 < pl.num_programs(0))
    def _():
        start(n0_ref, n1_ref, 1 - slot)

    _wait_rows(y_hbm, buf.at[slot, 0], sem.at[slot], tm)
    _wait_rows(y_hbm, buf.at[slot, 1], sem.at[slot], tm)
    pw = pw_ref[...]
    y = pw[:, 0:1] * buf[slot, 0] + pw[:, 1:2] * buf[slot, 1]
    o_ref[...] = _layer_norm(ALPHA * x_ref[...] + y, g_ref[...], b_ref[...])


def _combine_ln(ys, pos0, pos1, pw, x, g, b):
    m = x.shape[0]
    tm = _row_tile(m)
    n_tiles = m // tm
    row = lambda i: (i, 0)
    cur = lambda: pl.BlockSpec((1, 1, tm), lambda i: (i, 0, 0), memory_space=pltpu.SMEM)
    nxt = lambda: pl.BlockSpec((1, 1, tm), lambda i: (jnp.minimum(i + 1, n_tiles - 1), 0, 0),
                               memory_space=pltpu.SMEM)
    p0, p1 = pos0.reshape(n_tiles, 1, tm), pos1.reshape(n_tiles, 1, tm)
    return pl.pallas_call(
        functools.partial(_combine_body, tm=tm),
        grid=(n_tiles,),
        in_specs=[cur(), cur(), nxt(), nxt(), pl.BlockSpec(memory_space=pl.ANY), pl.BlockSpec((tm, 2), row),
                  pl.BlockSpec((tm, D_MODEL), row), _resident((1, D_MODEL)), _resident((1, D_MODEL))],
        out_specs=pl.BlockSpec((tm, D_MODEL), row),
        out_shape=jax.ShapeDtypeStruct((m, D_MODEL), F32),
        scratch_shapes=[pltpu.VMEM((2, 2, tm, D_MODEL), F32), pltpu.SemaphoreType.DMA((2,))],
        compiler_params=_cparams(1),
        name="moe_combine_ln",
    )(p0, p1, p0, p1, ys, pw, x, g.reshape(1, D_MODEL), b.reshape(1, D_MODEL))


def _moe(x, w_router, b_router, w_in, w_out, g, b):
    m = x.shape[0]
    tm = _row_tile(m)
    experts, probs = _router(x, w_router, b_router)
    e_flat = jnp.concatenate([experts[:, 0], experts[:, 1]])
    one_hot = (e_flat[:, None] == jnp.arange(N_EXPERTS, dtype=I32)[None, :]).astype(I32)
    csum = jnp.cumsum(one_hot, axis=0)
    rank = jnp.sum(one_hot * csum, axis=1) - 1
    counts = csum[-1]
    tiles_per_expert = (counts + tm - 1) // tm
    tile_end = jnp.cumsum(tiles_per_expert)
    start = (tile_end - tiles_per_expert) * tm
    pos = jnp.sum(one_hot * start[None, :], axis=1) + rank
    n_tiles = (2 * m) // tm + N_EXPERTS
    tile_expert = jnp.minimum(
        jnp.sum(jnp.arange(n_tiles, dtype=I32)[:, None] >= tile_end[None, :], axis=1), N_EXPERTS - 1
    ).astype(I32)
    n_active = tile_end[-1:].astype(I32)
    order = jnp.argsort(e_flat, stable=True).astype(I32)
    local = (jnp.arange(n_tiles * tm, dtype=I32).reshape(n_tiles, tm)
             - start[tile_expert][:, None])
    unpadded = (jnp.cumsum(counts) - counts)[tile_expert][:, None] + local
    valid = local < counts[tile_expert][:, None]
    src = jnp.where(valid, order[jnp.clip(unpadded, 0, 2 * m - 1)] % m, 0).reshape(-1)

    xs = _gather_rows(x, src, n_active, tm)
    ys = _ffn(xs, w_in, w_out, tile_expert, n_active, tm, name="moe_ffn")
    return _combine_ln(ys, pos[:m], pos[m:], probs, x, g, b)


def _rope_table_body(f_ref, cos_ref, sin_ref, *, past):
    pos = past + lax.broadcasted_iota(I32, cos_ref.shape, 0)
    ang = pos.astype(F32) * f_ref[...]
    lane = lax.broadcasted_iota(I32, cos_ref.shape, 1)
    cos_ref[...] = jnp.cos(ang)
    sin_ref[...] = jnp.where(lane < MLA_NOPE_DIM + MLA_ROPE_DIM // 2, -jnp.sin(ang), jnp.sin(ang))


def _rope_tables(t, past):
    half = MLA_ROPE_DIM // 2
    inv_freq = ROPE_THETA ** (-jnp.arange(half, dtype=F32) / half)
    f = jnp.zeros((1, LANES), F32)
    f = f.at[0, MLA_NOPE_DIM:MLA_NOPE_DIM + half].set(inv_freq)
    f = f.at[0, MLA_NOPE_DIM + half:MLA_QK_DIM].set(inv_freq)
    return pl.pallas_call(
        functools.partial(_rope_table_body, past=past),
        out_shape=[jax.ShapeDtypeStruct((t, LANES), F32)] * 2,
        name="rope_table",
    )(f)


def _rope_block(z, cosv, sinv):
    lane = lax.broadcasted_iota(I32, z.shape, 1)
    half = MLA_ROPE_DIM // 2
    partner = jnp.where(lane < MLA_NOPE_DIM + half,
                        pltpu.roll(z, LANES - half, 1), pltpu.roll(z, half, 1))
    return z * cosv + partner * sinv


def _mla_q_body(x_ref, wdq_ref, gq_ref, wuq_ref, wrot_ref, cos_ref, sin_ref, o_ref):
    cq = _rms(_dot(x_ref[...].astype(BF16), wdq_ref[...]), gq_ref[...]).astype(BF16)
    cosv = cos_ref[...]
    sinv = sin_ref[...]
    width = 4 * LANES
    for c0 in range(0, MLA_HEADS * LANES, width):
        r = _dot(cq, wuq_ref[:, c0:c0 + width])
        partner = _dot(cq, wrot_ref[:, c0:c0 + width])
        for j in range(0, width, LANES):
            blk = slice(j, j + LANES)
            o_ref[:, c0 + j:c0 + j + LANES] = (r[:, blk] * cosv + partner[:, blk] * sinv).astype(BF16)


def _pos_tables(tables, m, t):
    tm = _row_tile(m)
    if tm <= t:
        assert t % tm == 0
        n = t // tm
        return tables, pl.BlockSpec((tm, LANES), lambda i: (i % n, 0))
    assert tm % t == 0
    return [jnp.tile(tb, (tm // t, 1)) for tb in tables], pl.BlockSpec((tm, LANES), lambda i: (0, 0))


def _rotary_partner_cols(w_pad):
    k = w_pad.shape[0]
    w3 = w_pad.reshape(k, -1, LANES)
    half = MLA_ROPE_DIM // 2
    lane = jnp.arange(LANES)
    first = (lane >= MLA_NOPE_DIM) & (lane < MLA_NOPE_DIM + half)
    second = (lane >= MLA_NOPE_DIM + half) & (lane < MLA_QK_DIM)
    rot = jnp.where(first, jnp.roll(w3, -half, axis=2), jnp.where(second, jnp.roll(w3, half, axis=2), 0))
    return rot.reshape(w_pad.shape)


def _mla_q(x, w_dq, g_q, w_uq_pad, w_uq_rot, tables, t):
    m = x.shape[0]
    tm = _row_tile(m)
    (cos_t, sin_t), tspec = _pos_tables(tables, m, t)
    row = lambda i: (i, 0)
    return pl.pallas_call(
        _mla_q_body,
        grid=(m // tm,),
        in_specs=[pl.BlockSpec((tm, D_MODEL), row), _resident(w_dq.shape), _resident((1, MLA_Q_RANK)),
                  _resident(w_uq_pad.shape), _resident(w_uq_rot.shape), tspec, tspec],
        out_specs=pl.BlockSpec((tm, MLA_HEADS * LANES), row),
        out_shape=jax.ShapeDtypeStruct((m, MLA_HEADS * LANES), BF16),
        compiler_params=_cparams(1),
        name="mla_q",
    )(x, w_dq, g_q.reshape(1, MLA_Q_RANK), w_uq_pad, w_uq_rot, cos_t, sin_t)


def _mla_kv_down_body(x_ref, w_ref, g_ref, cos_ref, sin_ref, ckv_ref, kr_ref):
    kv = _dot(x_ref[...].astype(BF16), w_ref[...])
    ckv_ref[...] = _rms(kv[:, :MLA_KV_RANK], g_ref[...])
    kr_ref[...] = _rope_block(kv[:, MLA_KV_RANK:], cos_ref[...], sin_ref[...])


def _mla_kv_down(x, w_dkv_pad, g_kv, tables, t):
    m = x.shape[0]
    tm = _row_tile(m)
    (cos_t, sin_t), tspec = _pos_tables(tables, m, t)
    row = lambda i: (i, 0)
    return pl.pallas_call(
        _mla_kv_down_body,
        grid=(m // tm,),
        in_specs=[pl.BlockSpec((tm, D_MODEL), row), _resident(w_dkv_pad.shape),
                  _resident((1, MLA_KV_RANK)), tspec, tspec],
        out_specs=[pl.BlockSpec((tm, MLA_KV_RANK), row), pl.BlockSpec((tm, LANES), row)],
        out_shape=[jax.ShapeDtypeStruct((m, MLA_KV_RANK), F32), jax.ShapeDtypeStruct((m, LANES), F32)],
        compiler_params=_cparams(1),
        name="mla_kv_down",
    )(x, w_dkv_pad, g_kv.reshape(1, MLA_KV_RANK), cos_t, sin_t)


def _mla_kv_up_body(c_ref, kr_ref, wuk_ref, wuv_ref, k_ref, v_ref):
    cb = c_ref[...].astype(BF16)
    kr = kr_ref[...]
    width = 4 * LANES
    for c0 in range(0, MLA_HEADS * LANES, width):
        r = _dot(cb, wuk_ref[:, c0:c0 + width])
        for j in range(0, width, LANES):
            k_ref[:, c0 + j:c0 + j + LANES] = (r[:, j:j + LANES] + kr).astype(BF16)
    for c0 in range(0, MLA_HEADS * MLA_V_DIM, width):
        v_ref[:, c0:c0 + width] = _dot(cb, wuv_ref[:, c0:c0 + width]).astype(BF16)


def _mla_kv_up(ckv, kr_pad, w_uk_pad, w_uv):
    m = ckv.shape[0]
    tm = ROW_TILE if m % ROW_TILE == 0 else (LANES // 8 if m % LANES else LANES)
    assert m % tm == 0
    row = lambda i: (i, 0)
    return pl.pallas_call(
        _mla_kv_up_body,
        grid=(m // tm,),
        in_specs=[pl.BlockSpec((tm, MLA_KV_RANK), row), pl.BlockSpec((tm, LANES), row),
                  _resident(w_uk_pad.shape), _resident(w_uv.shape)],
        out_specs=[pl.BlockSpec((tm, MLA_HEADS * LANES), row), pl.BlockSpec((tm, MLA_HEADS * MLA_V_DIM), row)],
        out_shape=[jax.ShapeDtypeStruct((m, MLA_HEADS * LANES), BF16),
                   jax.ShapeDtypeStruct((m, MLA_HEADS * MLA_V_DIM), BF16)],
        compiler_params=_cparams(1),
        name="mla_kv_up",
    )(ckv, kr_pad, w_uk_pad, w_uv)


def _pool_body(prev_ref, halo_ref, x_ref, wp_ref, sc_ref, g_ref, b_ref, o_ref, ext, *, tm, pos0):
    i = pl.program_id(1)
    x = x_ref[0]
    ext[HALO:, :] = x

    @pl.when(i == 0)
    def _():
        ext[:HALO, :] = prev_ref[0]

    @pl.when(i > 0)
    def _():
        ext[:HALO, :] = halo_ref[0]

    pos = pos0 + i * tm + lax.broadcasted_iota(I32, (tm, 1), 0)
    for gi, w in enumerate(POOL_WINDOWS):
        ch = slice(gi * POOL_GROUP, (gi + 1) * POOL_GROUP)
        win = x[:, ch]
        for j in range(1, w):
            win = win + ext[HALO - j:HALO - j + tm, ch]
        count = jnp.minimum(pos + 1, w).astype(F32)
        mixed = (win / count - x[:, ch]).astype(BF16)
        o_ref[0, :, ch] = _dot(mixed, wp_ref[gi])
    y = o_ref[0] * sc_ref[...]
    o_ref[0] = _layer_norm(ALPHA * x + y, g_ref[...], b_ref[...])


def _pool_ln(x, prev_rows, pos0, w_pool, scale, g, b):
    bsz, t, _ = x.shape
    tm = _row_tile(t)
    prev = jnp.concatenate([jnp.zeros((bsz, HALO - POOL_STATE, D_MODEL), F32), prev_rows], axis=1)
    per = tm // HALO
    tile = lambda b_, i: (b_, i, 0)
    vec = pl.BlockSpec((1, D_MODEL), lambda b_, i: (0, 0))
    return pl.pallas_call(
        functools.partial(_pool_body, tm=tm, pos0=pos0),
        grid=(bsz, t // tm),
        in_specs=[pl.BlockSpec((1, HALO, D_MODEL), lambda b_, i: (b_, 0, 0)),
                  pl.BlockSpec((1, HALO, D_MODEL), lambda b_, i: (b_, jnp.maximum(i * per - 1, 0), 0)),
                  pl.BlockSpec((1, tm, D_MODEL), tile),
                  pl.BlockSpec(w_pool.shape, lambda b_, i: (0, 0, 0)), vec, vec, vec],
        out_specs=pl.BlockSpec((1, tm, D_MODEL), tile),
        out_shape=jax.ShapeDtypeStruct((bsz, t, D_MODEL), F32),
        scratch_shapes=[pltpu.VMEM((tm + HALO, D_MODEL), F32)],
        compiler_params=_cparams(2),
        name="pool_ln",
    )(prev, x, x, w_pool, scale.reshape(1, D_MODEL), g.reshape(1, D_MODEL), b.reshape(1, D_MODEL))


def _gla_proj_body(x_ref, win_ref, wa1_ref, wa2_ref, ba_ref, q_ref, k_ref, v_ref, gt_ref, la_ref):
    xb = x_ref[...].astype(BF16)
    width = 2 * MXU_DIM
    for o_ref, c0, c1 in ((q_ref, 0, GLA_KEY), (k_ref, GLA_KEY, 2 * GLA_KEY),
                          (v_ref, 2 * GLA_KEY, 2 * GLA_KEY + GLA_VAL),
                          (gt_ref, 2 * GLA_KEY + GLA_VAL, 2 * GLA_KEY + 2 * GLA_VAL)):
        for j0 in range(c0, c1, width):
            o_ref[:, j0 - c0:j0 - c0 + width] = _dot(xb, win_ref[:, j0:j0 + width]).astype(o_ref.dtype)
    low = _dot(xb, wa1_ref[...]).astype(BF16)
    z = _dot(low, wa2_ref[...]) + ba_ref[...]
    la_ref[...] = (jnp.minimum(z, 0.0) - jnp.log(1.0 + jnp.exp(-jnp.abs(z)))) / GLA_TAU


def _gla_proj(x, w_in, w_a1_pad, w_a2_pad, b_a):
    m = x.shape[0]
    tm = _row_tile(m)
    row = lambda i: (i, 0)
    widths = (GLA_KEY, GLA_KEY, GLA_VAL, GLA_VAL, GLA_KEY)
    dtypes = (F32, F32, BF16, F32, F32)
    return pl.pallas_call(
        _gla_proj_body,
        grid=(m // tm,),
        in_specs=[pl.BlockSpec((tm, D_MODEL), row), _resident(w_in.shape), _resident(w_a1_pad.shape),
                  _resident(w_a2_pad.shape), _resident((1, GLA_KEY))],
        out_specs=[pl.BlockSpec((tm, w), row) for w in widths],
        out_shape=[jax.ShapeDtypeStruct((m, w), d) for w, d in zip(widths, dtypes)],
        compiler_params=_cparams(1),
        name="gla_proj",
    )(x, w_in, w_a1_pad, w_a2_pad, b_a.reshape(1, GLA_KEY))


def _cumsum_rows(z):
    n = z.shape[0]
    row = lax.broadcasted_iota(I32, z.shape, 0)
    shift = 1
    while shift < n:
        z = z + jnp.where(row >= shift, pltpu.roll(z, shift, 0), 0.0)
        shift *= 2
    return z


def _gla_body(q_ref, k_ref, la_ref, v_ref, gt_ref, s0_ref, hg_ref, o_ref, s_ref, state, *,
              blk, n_blk, unroll):
    c = pl.program_id(1)

    @pl.when(c == 0)
    def _():
        state[...] = s0_ref[0]

    row = lax.broadcasted_iota(I32, (blk, blk), 0)
    col = lax.broadcasted_iota(I32, (blk, blk), 1)
    causal = row >= col

    def block(r0, h):
        rows = pl.ds(r0, blk)
        kc = slice(h * GLA_DK, (h + 1) * GLA_DK)
        vc = slice(h * GLA_DV, (h + 1) * GLA_DV)
        cum = _cumsum_rows(la_ref[0, rows, kc])
        cum_last = cum[blk - 1:blk, :]
        q = q_ref[0, rows, kc] * (GLA_DK ** -0.5)
        k = k_ref[0, rows, kc]
        v = v_ref[0, rows, vc]
        q_dec = (q * jnp.exp(cum)).astype(BF16)
        k_inv = (k * jnp.exp(-cum)).astype(BF16)
        k_end = (k * jnp.exp(cum_last - cum)).astype(BF16)
        scores = jnp.where(causal, _dot_nt(q_dec, k_inv), 0.0).astype(BF16)
        s_old = state[h]
        o = _dot(scores, v) + _dot_nt(q_dec, s_old.astype(BF16))
        state[h] = jnp.exp(cum_last) * s_old + _dot_tn(v, k_end)
        gate = gt_ref[0, rows, vc]
        o_ref[0, rows, vc] = (_rms(o, hg_ref[...]) * jax.nn.silu(gate)).astype(BF16)

    def step(j, carry):
        for u in range(unroll):
            r0 = pl.multiple_of((j * unroll + u) * blk, blk)
            for h in range(GLA_HEADS):
                block(r0, h)
        return carry

    lax.fori_loop(0, n_blk // unroll, step, 0)

    @pl.when(c == pl.num_programs(1) - 1)
    def _():
        s_ref[0] = state[...]


def _gla(q, k, la, v, gate, s0_t, head_g):
    bsz, t, _ = q.shape
    blk = GLA_BLOCK if t % GLA_BLOCK == 0 else t
    tc = _row_tile(t)
    n_blk = tc // blk
    kspec = pl.BlockSpec((1, tc, GLA_KEY), lambda b_, c: (b_, c, 0))
    vspec = pl.BlockSpec((1, tc, GLA_VAL), lambda b_, c: (b_, c, 0))
    sspec = pl.BlockSpec((1, GLA_HEADS, GLA_DV, GLA_DK), lambda b_, c: (b_, 0, 0, 0))
    return pl.pallas_call(
        functools.partial(_gla_body, blk=blk, n_blk=n_blk, unroll=2 if n_blk % 2 == 0 else 1),
        grid=(bsz, t // tc),
        in_specs=[kspec, kspec, kspec, vspec, vspec, sspec, pl.BlockSpec((1, GLA_DV), lambda b_, c: (0, 0))],
        out_specs=[vspec, sspec],
        out_shape=[jax.ShapeDtypeStruct((bsz, t, GLA_VAL), BF16),
                   jax.ShapeDtypeStruct((bsz, GLA_HEADS, GLA_DV, GLA_DK), F32)],
        scratch_shapes=[pltpu.VMEM((GLA_HEADS, GLA_DV, GLA_DK), F32)],
        compiler_params=_cparams(2),
        name="gla",
    )(q, k, la, v, gate, s0_t, head_g.reshape(1, GLA_DV))


def _pad_head_cols(w, heads, width):
    k = w.shape[0]
    w = w.reshape(k, heads, width)
    return jnp.pad(w, ((0, 0), (0, 0), (0, LANES - width))).reshape(k, heads * LANES)


def kernel(x_prompt, x_sample, cache_a_k, cache_a_v, cache_b_ckv, cache_b_krope, state_c_rows, state_d_gla, a_w_qkv, a_lam_q1, a_lam_k1, a_lam_q2, a_lam_k2, a_subln_g, a_w_o, b_w_dq, b_q_norm_g, b_w_uq, b_w_dkv, b_kv_norm_g, b_w_uk, b_w_uv, b_w_o, c_w_pool, c_scale, d_w_in, d_w_a1, d_w_a2, d_b_a, d_head_g, d_w_o, ln_g, ln_b, ffn_w_in, ffn_w_out, moe_w_router, moe_b_router, moe_w_in, moe_w_out):
    bf = lambda w: w.astype(BF16)
    d = D_MODEL
    past = cache_a_k.shape[1]
    streams = [(x_prompt.shape[0], x_prompt.shape[1], 0), (x_sample.shape[0], x_sample.shape[1], past)]
    xs = [x_prompt.reshape(-1, d), x_sample.reshape(-1, d)]

    ffn_in, ffn_out = bf(ffn_w_in), bf(ffn_w_out)
    moe_in, moe_out = bf(moe_w_in), bf(moe_w_out)

    w_qkv, w_o = bf(a_w_qkv), bf(a_w_o)
    lam_vecs = jnp.stack([a_lam_q1, a_lam_k1, a_lam_q2, a_lam_k2]).astype(F32)
    lambda_init = 0.8 - 0.6 * math.exp(-0.3 * 0)
    caches = [(None, None), (cache_a_k, cache_a_v)]
    a_k, a_v = [], []
    for s, ((bsz, t, p), (ck, cv)) in enumerate(zip(streams, caches)):
        qb, kf, kb, vf, vb = _proj(
            xs[s], w_qkv,
            [(0, d, [(BF16, DA_HEAD_DIM ** -0.5)]), (d, 2 * d, [(F32, 1.0), (BF16, 1.0)]),
             (2 * d, 3 * d, [(F32, 1.0), (BF16, 1.0)])], name="a_qkv")
        kb, vb = kb.reshape(bsz, t, d), vb.reshape(bsz, t, d)
        if p:
            kb = jnp.concatenate([bf(ck.reshape(bsz, p, d)), kb], axis=1)
            vb = jnp.concatenate([bf(cv.reshape(bsz, p, d)), vb], axis=1)
        o = _attention("diff", qb.reshape(bsz, t, d), kb, vb, lam_vecs=lam_vecs, subln_g=a_subln_g,
                       lambda_init=lambda_init, name="a_attn")
        xs[s] = _oproj_ln(o.reshape(-1, d), w_o, xs[s], ln_g[0, 0], ln_b[0, 0], name="a_out_ln")
        xs[s] = _dense_ffn(xs[s], ffn_in, ffn_out, 0, ln_g[0, 1], ln_b[0, 1], name="ffn0")
        a_k.append(kf.reshape(bsz, t, DA_HEADS, 2, DA_HEAD_DIM))
        a_v.append(vf.reshape(bsz, t, DA_HEADS, 2 * DA_HEAD_DIM))

    w_dq, w_o = bf(b_w_dq), bf(b_w_o)
    w_uq_pad = bf(_pad_head_cols(b_w_uq, MLA_HEADS, MLA_QK_DIM))
    w_uq_rot = _rotary_partner_cols(w_uq_pad)
    w_uk_pad = bf(_pad_head_cols(b_w_uk, MLA_HEADS, MLA_NOPE_DIM))
    w_uv = bf(b_w_uv)
    w_dkv_pad = bf(jnp.concatenate(
        [b_w_dkv[:, :MLA_KV_RANK], jnp.zeros((d, MLA_NOPE_DIM), F32), b_w_dkv[:, MLA_KV_RANK:],
         jnp.zeros((d, LANES - MLA_QK_DIM), F32)], axis=1))
    caches = [(None, None), (cache_b_ckv, cache_b_krope)]
    b_ckv, b_kr = [], []
    for s, ((bsz, t, p), (cc, ckr)) in enumerate(zip(streams, caches)):
        tables = _rope_tables(t, p)
        q = _mla_q(xs[s], w_dq, b_q_norm_g, w_uq_pad, w_uq_rot, tables, t)
        ckv, kr_pad = _mla_kv_down(xs[s], w_dkv_pad, b_kv_norm_g, tables, t)
        ckv_all, kr_all = ckv, kr_pad
        if p:
            ckr_pad = jnp.pad(ckr, ((0, 0), (0, 0), (MLA_NOPE_DIM, LANES - MLA_QK_DIM)))
            ckv_all = jnp.concatenate([cc, ckv.reshape(bsz, t, -1)], axis=1).reshape(-1, MLA_KV_RANK)
            kr_all = jnp.concatenate([ckr_pad, kr_pad.reshape(bsz, t, -1)], axis=1).reshape(-1, LANES)
        k_pad, v = _mla_kv_up(ckv_all, kr_all, w_uk_pad, w_uv)
        o = _attention("pair", q.reshape(bsz, t, -1), k_pad.reshape(bsz, p + t, -1),
                       v.reshape(bsz, p + t, -1), scale=MLA_QK_DIM ** -0.5, name="b_attn")
        xs[s] = _oproj_ln(o.reshape(-1, d), w_o, xs[s], ln_g[1, 0], ln_b[1, 0], name="b_out_ln")
        xs[s] = _moe(xs[s], moe_w_router[0], moe_b_router[0], moe_in[0], moe_out[0], ln_g[1, 1], ln_b[1, 1])
        b_ckv.append(ckv.reshape(bsz, t, MLA_KV_RANK))
        b_kr.append(kr_pad[:, MLA_NOPE_DIM:MLA_QK_DIM].reshape(bsz, t, MLA_ROPE_DIM))

    w_pool = bf(c_w_pool)
    prevs = [jnp.zeros((streams[0][0], POOL_STATE, d), F32), state_c_rows]
    c_rows = []
    for s, ((bsz, t, p), prev) in enumerate(zip(streams, prevs)):
        x3 = xs[s].reshape(bsz, t, d)
        c_rows.append(jnp.concatenate([prev, x3], axis=1)[:, -POOL_STATE:])
        xs[s] = _pool_ln(x3, prev, p, w_pool, c_scale, ln_g[2, 0], ln_b[2, 0]).reshape(-1, d)
        xs[s] = _dense_ffn(xs[s], ffn_in, ffn_out, 1, ln_g[2, 1], ln_b[2, 1], name="ffn1")

    w_in, w_o = bf(d_w_in), bf(d_w_o)
    w_a1_pad = bf(jnp.pad(d_w_a1, ((0, 0), (0, LANES - GLA_GATE_RANK))))
    w_a2_pad = bf(jnp.pad(d_w_a2, ((0, LANES - GLA_GATE_RANK), (0, 0))))
    states = [jnp.zeros((streams[0][0], GLA_HEADS, GLA_DK, GLA_DV), F32), state_d_gla]
    d_gla = []
    for s, ((bsz, t, p), s0) in enumerate(zip(streams, states)):
        q, k, v, gate, la = _gla_proj(xs[s], w_in, w_a1_pad, w_a2_pad, d_b_a)
        shp = lambda z: z.reshape(bsz, t, -1)
        o, s_t = _gla(shp(q), shp(k), shp(la), shp(v), shp(gate), jnp.swapaxes(s0, 2, 3), d_head_g)
        xs[s] = _oproj_ln(o.reshape(-1, d), w_o, xs[s], ln_g[3, 0], ln_b[3, 0], name="d_out_ln")
        xs[s] = _moe(xs[s], moe_w_router[1], moe_b_router[1], moe_in[1], moe_out[1], ln_g[3, 1], ln_b[3, 1])
        d_gla.append(jnp.swapaxes(s_t, 2, 3))

    y = [xs[s].reshape(bsz, t, d) for s, (bsz, t, _) in enumerate(streams)]
    return (y[0], y[1], a_k[0], a_v[0], a_k[1], a_v[1], b_ckv[0], b_kr[0], b_ckv[1], b_kr[1],
            c_rows[0], c_rows[1], d_gla[0], d_gla[1])
```

```python
import functools
import math

import jax
import jax.numpy as jnp
from jax import lax
from jax.experimental import pallas as pl
from jax.experimental.pallas import tpu as pltpu

F32 = jnp.float32
BF16 = jnp.bfloat16
I32 = jnp.int32

D_MODEL = 1024
DEPTH = 4
CHUNK = 64
ALPHA = (2 * DEPTH) ** 0.25
LN_EPS = 1e-5
RMS_EPS = 1e-6
NEG_INF = -1e30
LOG2E = math.log2(math.e)

DA_HEADS = 8
DA_HEAD_DIM = 64
MLA_HEADS = 16
MLA_NOPE_DIM = 64
MLA_ROPE_DIM = 32
MLA_QK_DIM = MLA_NOPE_DIM + MLA_ROPE_DIM
MLA_V_DIM = 64
MLA_Q_RANK = 384
MLA_KV_RANK = 256
ROPE_THETA = 10000.0
POOL_WINDOWS = (2, 4, 8, 16)
POOL_GROUP = D_MODEL // len(POOL_WINDOWS)
POOL_STATE = max(POOL_WINDOWS) - 1
GLA_HEADS = 4
GLA_DK = 128
GLA_DV = 256
GLA_KEY = GLA_HEADS * GLA_DK
GLA_VAL = GLA_HEADS * GLA_DV
GLA_GATE_RANK = 16
GLA_TAU = 16.0
GLA_BLOCK = CHUNK
N_EXPERTS = 8

LANES = 128
MXU_DIM = 256
VMEM_LIMIT_BYTES = 56 * 1024 * 1024
ROW_TILE = 512
HALO = 16
DMA_UNROLL = 8


def _cparams(n_axes):
    return pltpu.CompilerParams(
        dimension_semantics=("arbitrary",) * n_axes, vmem_limit_bytes=VMEM_LIMIT_BYTES)


def _row_tile(m):
    return ROW_TILE if m % ROW_TILE == 0 else m


def _resident(shape):
    nd = len(shape)
    return pl.BlockSpec(shape, lambda *_: (0,) * nd)


def _layer_norm(z, g, b):
    mu = jnp.mean(z, axis=-1, keepdims=True)
    zc = z - mu
    var = jnp.mean(zc * zc, axis=-1, keepdims=True)
    return zc * lax.rsqrt(var + LN_EPS) * g + b


def _rms(z, g):
    return z * lax.rsqrt(jnp.mean(z * z, axis=-1, keepdims=True) + RMS_EPS) * g


def _dot(a, b):
    return jnp.dot(a, b, preferred_element_type=F32)


def _dot_nt(a, b):
    return lax.dot_general(a, b, (((1,), (1,)), ((), ())), preferred_element_type=F32)


def _dot_tn(a, b):
    return lax.dot_general(a, b, (((0,), (0,)), ((), ())), preferred_element_type=F32)


def _proj_body(x_ref, w_ref, *refs, groups, chunk, transposed):
    xb = x_ref[...].astype(BF16)
    if transposed:
        wt_ref, *refs, ot_ref = refs
        for r0 in range(0, wt_ref.shape[0], chunk):
            ot_ref[0, r0:r0 + chunk, :] = _dot_nt(wt_ref[r0:r0 + chunk, :], xb)
    o_refs = iter(refs)
    for c0, c1, sinks in groups:
        outs = [next(o_refs) for _ in sinks]
        for j0 in range(c0, c1, chunk):
            j1 = min(j0 + chunk, c1)
            r = _dot(xb, w_ref[:, j0:j1])
            for o_ref, (_, sc) in zip(outs, sinks):
                o_ref[:, j0 - c0:j1 - c0] = (r if sc == 1.0 else r * sc).astype(o_ref.dtype)


def _proj(x, w, groups, name, transposed=None):
    m, k = x.shape
    tm = _row_tile(m)
    outs = [(c1 - c0, dt) for c0, c1, sinks in groups for dt, _ in sinks]
    in_specs = [pl.BlockSpec((tm, k), lambda i: (i, 0)), _resident(w.shape)]
    args = [x, w]
    out_specs = [pl.BlockSpec((tm, wd), lambda i: (i, 0)) for wd, _ in outs]
    out_shape = [jax.ShapeDtypeStruct((m, wd), dt) for wd, dt in outs]
    if transposed is not None:
        w_t, bsz, t = transposed
        assert t % tm == 0 and bsz * t == m
        per = t // tm
        in_specs.append(_resident(w_t.shape))
        args.append(w_t)
        out_specs.append(pl.BlockSpec((1, w_t.shape[0], tm), lambda i: (i // per, 0, i % per)))
        out_shape.append(jax.ShapeDtypeStruct((bsz, w_t.shape[0], t), F32))
    return pl.pallas_call(
        functools.partial(_proj_body, groups=groups, chunk=2 * MXU_DIM, transposed=transposed is not None),
        grid=(m // tm,),
        in_specs=in_specs,
        out_specs=out_specs,
        out_shape=out_shape,
        compiler_params=_cparams(1),
        name=name,
    )(*args)


def _oproj_ln_body(a_ref, w_ref, x_ref, g_ref, b_ref, o_ref):
    y = _dot(a_ref[...], w_ref[...])
    o_ref[...] = _layer_norm(ALPHA * x_ref[...] + y, g_ref[...], b_ref[...])


def _oproj_ln(a, w, x, g, b, name):
    m, kd = a.shape
    tm = _row_tile(m)
    row = lambda i: (i, 0)
    return pl.pallas_call(
        _oproj_ln_body,
        grid=(m // tm,),
        in_specs=[pl.BlockSpec((tm, kd), row), _resident(w.shape), pl.BlockSpec((tm, D_MODEL), row),
                  _resident((1, D_MODEL)), _resident((1, D_MODEL))],
        out_specs=pl.BlockSpec((tm, D_MODEL), row),
        out_shape=jax.ShapeDtypeStruct((m, D_MODEL), F32),
        compiler_params=_cparams(1),
        name=name,
    )(a, w, x, g.reshape(1, D_MODEL), b.reshape(1, D_MODEL))


def _attn_plan(tq_total, tk_valid):
    past = tk_valid - tq_total
    if past == 0:
        t = min(ROW_TILE, tq_total)
        assert tq_total % t == 0 and t % LANES == 0
        return t, t, tq_total // t, None, t, tk_valid
    tq = tq_total
    first_chunk = past // CHUNK
    all_visible = min((first_chunk + 1) * CHUNK, tk_valid)
    tk = ROW_TILE
    n_full = all_visible // tk
    last_chunk = (past + tq - 1) // CHUNK
    end = min((last_chunk + 1) * CHUNK, tk_valid)
    tail = -(-(end - n_full * tk) // LANES) * LANES
    return tq, tk, 1, n_full, tail, max(n_full * tk + tail, tk_valid)


def _attn_body(*refs, mode, tq, tk, n_full_static, tail, past, tk_valid, scale, lambda_init):
    if mode == "diff":
        lam_ref, q_ref, k_ref, v_ref, g_ref, o_ref, m_sc, l_sc, acc_sc = refs
    else:
        q_ref, k_ref, v_ref, o_ref, m_sc, l_sc, acc_sc = refs
    qi = pl.program_id(2)
    n_full = qi if n_full_static is None else n_full_static
    rows = 2 * tq
    exp = jnp.exp if mode == "diff" else jnp.exp2
    lane = lax.broadcasted_iota(I32, (rows, LANES), 1)
    lo = lax.broadcasted_iota(I32, (tq, LANES), 1) < LANES // 2
    lo_row = lax.broadcasted_iota(I32, (1, LANES), 1) < LANES // 2

    q = q_ref[0]
    if mode == "diff":
        zero = jnp.zeros_like(q)
        q_both = jnp.concatenate([jnp.where(lo, q, zero), jnp.where(lo, zero, q)], axis=0)
    else:
        q0 = q[:, :LANES]
        q1 = q[:, LANES:]

    m_sc[...] = jnp.full(m_sc.shape, NEG_INF, F32)
    l_sc[...] = jnp.zeros(l_sc.shape, F32)
    acc_sc[...] = jnp.zeros(acc_sc.shape, F32)

    def tile(off, length, masked):
        k = k_ref[0, pl.ds(off, length), :]
        v = v_ref[0, pl.ds(off, length), :]
        if mode == "diff":
            s = _dot_nt(q_both, k)
        else:
            s = jnp.concatenate([_dot_nt(q0, k[:, :LANES]), _dot_nt(q1, k[:, LANES:])], axis=0)
            s = s * (scale * LOG2E)
        blocks = [s[:, c:c + LANES] for c in range(0, length, LANES)]
        if masked:
            row = lax.broadcasted_iota(I32, (rows, LANES), 0)
            q_pos = past + qi * tq + jnp.where(row >= tq, row - tq, row)
            limit = jnp.minimum((q_pos // CHUNK + 1) * CHUNK, tk_valid) - off
            blocks = [jnp.where(lane + c * LANES < limit, blk, NEG_INF) for c, blk in enumerate(blocks)]
        m_old = m_sc[...]
        m_new = jnp.maximum(
            m_old, jnp.max(functools.reduce(jnp.maximum, blocks), axis=-1, keepdims=True))
        alpha = exp(m_old - m_new)
        probs = [exp(blk - m_new) for blk in blocks]
        l_sc[...] = alpha * l_sc[...] + functools.reduce(jnp.add, probs)
        m_sc[...] = m_new
        pb = jnp.concatenate([p.astype(BF16) for p in probs], axis=1)
        if mode == "diff":
            acc_sc[...] = alpha * acc_sc[...] + _dot(pb, v)
        else:
            zero = jnp.zeros_like(v)
            v0 = jnp.where(lo_row, v, zero)
            v1 = jnp.where(lo_row, zero, v)
            alpha_l = jnp.where(lo, alpha[:tq], alpha[tq:])
            acc_sc[...] = alpha_l * acc_sc[...] + _dot(pb[:tq], v0) + _dot(pb[tq:], v1)

    def full_tile(j, carry):
        tile(pl.multiple_of(j * tk, tk), tk, False)
        return carry

    lax.fori_loop(0, n_full, full_tile, 0)
    tile(n_full * tk if n_full_static is not None else pl.multiple_of(n_full * tk, tk), tail, True)

    inv_l = 1.0 / jnp.sum(l_sc[...], axis=-1, keepdims=True)
    if mode == "diff":
        lam_v = lam_ref[...]
        lam = (jnp.exp(jnp.sum(lam_v[0:1] * lam_v[1:2], keepdims=True))
               - jnp.exp(jnp.sum(lam_v[2:3] * lam_v[3:4], keepdims=True)) + lambda_init)
        acc = acc_sc[...]
        o = acc[:tq] * inv_l[:tq] - lam * (acc[tq:] * inv_l[tq:])
        o = _rms(o, g_ref[...]) * (1.0 - lambda_init)
    else:
        o = acc_sc[...] * jnp.where(lo, inv_l[:tq], inv_l[tq:])
    o_ref[0] = o.astype(o_ref.dtype)


def _attention(mode, q, k, v, *, lam_vecs=None, subln_g=None, lambda_init=0.0, scale=1.0, name):
    bsz, tq_total, _ = q.shape
    tk_valid = k.shape[1]
    qw = LANES if mode == "diff" else 2 * LANES
    heads = q.shape[2] // qw
    tq, tk, nq, n_full, tail, tk_pad = _attn_plan(tq_total, tk_valid)
    if tk_pad > tk_valid:
        k = jnp.pad(k, ((0, 0), (0, tk_pad - tk_valid), (0, 0)))
        v = jnp.pad(v, ((0, 0), (0, tk_pad - tk_valid), (0, 0)))
    body = functools.partial(
        _attn_body, mode=mode, tq=tq, tk=tk, n_full_static=n_full, tail=tail,
        past=tk_valid - tq_total, tk_valid=tk_valid, scale=scale, lambda_init=lambda_init)
    in_specs = [
        pl.BlockSpec((1, tq, qw), lambda b, h, i: (b, i, h)),
        pl.BlockSpec((1, tk_pad, qw), lambda b, h, i: (b, 0, h)),
        pl.BlockSpec((1, tk_pad, LANES), lambda b, h, i: (b, 0, h)),
    ]
    args = [q, k, v]
    if mode == "diff":
        in_specs = [_resident((4, DA_HEAD_DIM))] + in_specs + [_resident((1, LANES))]
        args = [lam_vecs] + args + [subln_g.reshape(1, LANES)]
    return pl.pallas_call(
        body,
        grid=(bsz, heads, nq),
        in_specs=in_specs,
        out_specs=pl.BlockSpec((1, tq, LANES), lambda b, h, i: (b, i, h)),
        out_shape=jax.ShapeDtypeStruct((bsz, tq_total, heads * LANES), BF16),
        scratch_shapes=[pltpu.VMEM((2 * tq, LANES), F32), pltpu.VMEM((2 * tq, LANES), F32),
                        pltpu.VMEM((2 * tq if mode == "diff" else tq, LANES), F32)],
        compiler_params=_cparams(3),
        name=name,
    )(*args)


def _ffn_body(te_ref, na_ref, x_ref, win_ref, wout_ref, *rest, dff, fused_ln):
    if fused_ln:
        g_ref, b_ref, o_ref = rest
    else:
        (o_ref,) = rest
    i = pl.program_id(0)

    @pl.when(i < na_ref[0])
    def _():
        x = x_ref[...]
        xb = x.astype(BF16)
        acc = jnp.zeros((x.shape[0], D_MODEL), F32)
        for c0 in range(0, dff, MXU_DIM):
            c1 = min(c0 + MXU_DIM, dff)
            gate = _dot(xb, win_ref[0, :, c0:c1])
            up = _dot(xb, win_ref[0, :, dff + c0:dff + c1])
            act = (jax.nn.silu(gate) * up).astype(BF16)
            acc = acc + _dot(act, wout_ref[0, c0:c1, :])
        if fused_ln:
            acc = _layer_norm(ALPHA * x + acc, g_ref[...], b_ref[...])
        o_ref[...] = acc

    @pl.when(i >= na_ref[0])
    def _():
        o_ref[...] = jnp.zeros(o_ref.shape, F32)


def _ffn(x, w_in, w_out, tile_expert, n_active, tm, ln=None, name="ffn"):
    m = x.shape[0]
    dff = w_out.shape[1]
    row_spec = pl.BlockSpec((tm, D_MODEL), lambda i, te, na: (i, 0))
    in_specs = [
        row_spec,
        pl.BlockSpec((1, D_MODEL, 2 * dff), lambda i, te, na: (te[i], 0, 0)),
        pl.BlockSpec((1, dff, D_MODEL), lambda i, te, na: (te[i], 0, 0)),
    ]
    args = [x, w_in, w_out]
    if ln is not None:
        in_specs += [pl.BlockSpec((1, D_MODEL), lambda i, te, na: (0, 0))] * 2
        args += [ln[0].reshape(1, D_MODEL), ln[1].reshape(1, D_MODEL)]
    return pl.pallas_call(
        functools.partial(_ffn_body, dff=dff, fused_ln=ln is not None),
        grid_spec=pltpu.PrefetchScalarGridSpec(
            num_scalar_prefetch=2,
            grid=(m // tm,),
            in_specs=in_specs,
            out_specs=row_spec,
        ),
        out_shape=jax.ShapeDtypeStruct(x.shape, F32),
        compiler_params=_cparams(1),
        name=name,
    )(tile_expert, n_active, *args)


def _dense_ffn(x, w_in, w_out, layer, g, b, name):
    tm = _row_tile(x.shape[0])
    n_tiles = x.shape[0] // tm
    te = jnp.full((n_tiles,), layer, I32)
    return _ffn(x, w_in, w_out, te, jnp.full((1,), n_tiles, I32), tm, ln=(g, b), name=name)


def _router_body(x_ref, w_ref, b_ref, e_ref, p_ref):
    logits = jnp.dot(x_ref[...], w_ref[...], preferred_element_type=F32,
                     precision=lax.Precision.HIGHEST) + b_ref[...]
    lane = lax.broadcasted_iota(I32, logits.shape, 1).astype(F32)
    v1 = jnp.max(logits, axis=-1, keepdims=True)
    i1 = jnp.min(jnp.where(logits == v1, lane, float(LANES)), axis=-1, keepdims=True)
    rest = jnp.where(lane == i1, NEG_INF, logits)
    v2 = jnp.max(rest, axis=-1, keepdims=True)
    i2 = jnp.min(jnp.where(rest == v2, lane, float(LANES)), axis=-1, keepdims=True)
    e2 = jnp.exp(v2 - v1)
    den = 1.0 + e2
    col = lax.broadcasted_iota(I32, e_ref.shape, 1)
    e_ref[...] = jnp.where(col == 0, i1, i2).astype(I32)
    p_ref[...] = jnp.where(col == 0, 1.0 / den, e2 / den)


def _router(x, w_router, b_router):
    m = x.shape[0]
    tm = _row_tile(m)
    w_pad = jnp.zeros((D_MODEL, LANES), F32).at[:, :N_EXPERTS].set(w_router)
    b_pad = jnp.full((1, LANES), NEG_INF, F32).at[0, :N_EXPERTS].set(b_router)
    row = lambda i: (i, 0)
    return pl.pallas_call(
        _router_body,
        grid=(m // tm,),
        in_specs=[pl.BlockSpec((tm, D_MODEL), row), _resident((D_MODEL, LANES)), _resident((1, LANES))],
        out_specs=[pl.BlockSpec((tm, 2), row), pl.BlockSpec((tm, 2), row)],
        out_shape=[jax.ShapeDtypeStruct((m, 2), I32), jax.ShapeDtypeStruct((m, 2), F32)],
        compiler_params=_cparams(1),
        name="moe_router",
    )(x, w_pad, b_pad)


def _row_copy(src, src_row, dst, dst_row, sem):
    return pltpu.make_async_copy(src.at[pl.ds(src_row, 1)], dst.at[pl.ds(dst_row, 1)], sem)


def _start_rows(src_hbm, idx_ref, dst, sem, tm):
    def issue(j, carry):
        for u in range(DMA_UNROLL):
            r = j * DMA_UNROLL + u
            _row_copy(src_hbm, idx_ref[0, 0, r], dst, r, sem).start(priority=u % 2)
        return carry

    lax.fori_loop(0, tm // DMA_UNROLL, issue, 0)


def _wait_rows(src_hbm, dst, sem, tm):
    def drain(j, carry):
        for _ in range(DMA_UNROLL):
            _row_copy(src_hbm, 0, dst, 0, sem).wait()
        return carry

    lax.fori_loop(0, tm // DMA_UNROLL, drain, 0)


def _gather_body(na_ref, src_ref, nxt_ref, x_hbm, o_ref, buf, sem, *, tm):
    i = pl.program_id(0)
    slot = i % 2

    @pl.when(i == 0)
    def _():
        _start_rows(x_hbm, src_ref, buf.at[0], sem.at[0], tm)

    @pl.when(i + 1 < na_ref[0])
    def _():
        _start_rows(x_hbm, nxt_ref, buf.at[1 - slot], sem.at[1 - slot], tm)

    @pl.when(i < na_ref[0])
    def _():
        _wait_rows(x_hbm, buf.at[slot], sem.at[slot], tm)
        o_ref[...] = buf[slot]

    @pl.when(i >= na_ref[0])
    def _():
        o_ref[...] = jnp.zeros(o_ref.shape, F32)


def _gather_rows(x, src, n_active, tm):
    n_rows = src.shape[0]
    n_tiles = n_rows // tm
    src3 = src.reshape(n_tiles, 1, tm)
    tile = (tm, D_MODEL)
    return pl.pallas_call(
        functools.partial(_gather_body, tm=tm),
        grid_spec=pltpu.PrefetchScalarGridSpec(
            num_scalar_prefetch=1,
            grid=(n_tiles,),
            in_specs=[pl.BlockSpec((1, 1, tm), lambda i, na: (i, 0, 0), memory_space=pltpu.SMEM),
                      pl.BlockSpec((1, 1, tm), lambda i, na: (jnp.minimum(i + 1, n_tiles - 1), 0, 0),
                                   memory_space=pltpu.SMEM),
                      pl.BlockSpec(memory_space=pl.ANY)],
            out_specs=pl.BlockSpec(tile, lambda i, na: (i, 0)),
            scratch_shapes=[pltpu.VMEM((2,) + tile, F32), pltpu.SemaphoreType.DMA((2,))],
        ),
        out_shape=jax.ShapeDtypeStruct((n_rows, D_MODEL), F32),
        compiler_params=_cparams(1),
        name="moe_dispatch",
    )(n_active, src3, src3, x)


def _combine_body(p0_ref, p1_ref, n0_ref, n1_ref, y_hbm, pw_ref, x_ref, g_ref, b_ref, o_ref, buf, sem,
                  *, tm):
    i = pl.program_id(0)
    slot = i % 2

    def start(q0_ref, q1_ref, s):
        _start_rows(y_hbm, q0_ref, buf.at[s, 0], sem.at[s], tm)
        _start_rows(y_hbm, q1_ref, buf.at[s, 1], sem.at[s], tm)

    @pl.when(i == 0)
    def _():
        start(p0_ref, p1_ref, 0)

    @pl.when(i + 1 < pl.num_programs(0))
    def _():
        start(n0_ref, n1_ref, 1 - slot)

    _wait_rows(y_hbm, buf.at[slot, 0], sem.at[slot], tm)
    _wait_rows(y_hbm, buf.at[slot, 1], sem.at[slot], tm)
    pw = pw_ref[...]
    y = pw[:, 0:1] * buf[slot, 0] + pw[:, 1:2] * buf[slot, 1]
    o_ref[...] = _layer_norm(ALPHA * x_ref[...] + y, g_ref[...], b_ref[...])


def _combine_ln(ys, pos0, pos1, pw, x, g, b):
    m = x.shape[0]
    tm = _row_tile(m)
    n_tiles = m // tm
    row = lambda i: (i, 0)
    cur = lambda: pl.BlockSpec((1, 1, tm), lambda i: (i, 0, 0), memory_space=pltpu.SMEM)
    nxt = lambda: pl.BlockSpec((1, 1, tm), lambda i: (jnp.minimum(i + 1, n_tiles - 1), 0, 0),
                               memory_space=pltpu.SMEM)
    p0, p1 = pos0.reshape(n_tiles, 1, tm), pos1.reshape(n_tiles, 1, tm)
    return pl.pallas_call(
        functools.partial(_combine_body, tm=tm),
        grid=(n_tiles,),
        in_specs=[cur(), cur(), nxt(), nxt(), pl.BlockSpec(memory_space=pl.ANY), pl.BlockSpec((tm, 2), row),
                  pl.BlockSpec((tm, D_MODEL), row), _resident((1, D_MODEL)), _resident((1, D_MODEL))],
        out_specs=pl.BlockSpec((tm, D_MODEL), row),
        out_shape=jax.ShapeDtypeStruct((m, D_MODEL), F32),
        scratch_shapes=[pltpu.VMEM((2, 2, tm, D_MODEL), F32), pltpu.SemaphoreType.DMA((2,))],
        compiler_params=_cparams(1),
        name="moe_combine_ln",
    )(p0, p1, p0, p1, ys, pw, x, g.reshape(1, D_MODEL), b.reshape(1, D_MODEL))


def _moe(x, w_router, b_router, w_in, w_out, g, b):
    m = x.shape[0]
    tm = _row_tile(m)
    experts, probs = _router(x, w_router, b_router)
    e_flat = jnp.concatenate([experts[:, 0], experts[:, 1]])
    one_hot = (e_flat[:, None] == jnp.arange(N_EXPERTS, dtype=I32)[None, :]).astype(I32)
    csum = jnp.cumsum(one_hot, axis=0)
    rank = jnp.sum(one_hot * csum, axis=1) - 1
    counts = csum[-1]
    tiles_per_expert = (counts + tm - 1) // tm
    tile_end = jnp.cumsum(tiles_per_expert)
    start = (tile_end - tiles_per_expert) * tm
    pos = jnp.sum(one_hot * start[None, :], axis=1) + rank
    n_tiles = (2 * m) // tm + N_EXPERTS
    tile_expert = jnp.minimum(
        jnp.sum(jnp.arange(n_tiles, dtype=I32)[:, None] >= tile_end[None, :], axis=1), N_EXPERTS - 1
    ).astype(I32)
    n_active = tile_end[-1:].astype(I32)
    order = jnp.argsort(e_flat, stable=True).astype(I32)
    local = (jnp.arange(n_tiles * tm, dtype=I32).reshape(n_tiles, tm)
             - start[tile_expert][:, None])
    unpadded = (jnp.cumsum(counts) - counts)[tile_expert][:, None] + local
    valid = local < counts[tile_expert][:, None]
    src = jnp.where(valid, order[jnp.clip(unpadded, 0, 2 * m - 1)] % m, 0).reshape(-1)

    xs = _gather_rows(x, src, n_active, tm)
    ys = _ffn(xs, w_in, w_out, tile_expert, n_active, tm, name="moe_ffn")
    return _combine_ln(ys, pos[:m], pos[m:], probs, x, g, b)


def _rope_table_body(f_ref, cos_ref, sin_ref, *, past):
    pos = past + lax.broadcasted_iota(I32, cos_ref.shape, 0)
    ang = pos.astype(F32) * f_ref[...]
    lane = lax.broadcasted_iota(I32, cos_ref.shape, 1)
    cos_ref[...] = jnp.cos(ang)
    sin_ref[...] = jnp.where(lane < MLA_NOPE_DIM + MLA_ROPE_DIM // 2, -jnp.sin(ang), jnp.sin(ang))


def _rope_tables(t, past):
    half = MLA_ROPE_DIM // 2
    inv_freq = ROPE_THETA ** (-jnp.arange(half, dtype=F32) / half)
    f = jnp.zeros((1, LANES), F32)
    f = f.at[0, MLA_NOPE_DIM:MLA_NOPE_DIM + half].set(inv_freq)
    f = f.at[0, MLA_NOPE_DIM + half:MLA_QK_DIM].set(inv_freq)
    return pl.pallas_call(
        functools.partial(_rope_table_body, past=past),
        out_shape=[jax.ShapeDtypeStruct((t, LANES), F32)] * 2,
        name="rope_table",
    )(f)


def _rope_block(z, cosv, sinv):
    lane = lax.broadcasted_iota(I32, z.shape, 1)
    half = MLA_ROPE_DIM // 2
    partner = jnp.where(lane < MLA_NOPE_DIM + half,
                        pltpu.roll(z, LANES - half, 1), pltpu.roll(z, half, 1))
    return z * cosv + partner * sinv


def _mla_q_body(x_ref, wdq_ref, gq_ref, wuq_ref, wrot_ref, cos_ref, sin_ref, o_ref):
    cq = _rms(_dot(x_ref[...].astype(BF16), wdq_ref[...]), gq_ref[...]).astype(BF16)
    cosv = cos_ref[...]
    sinv = sin_ref[...]
    width = 4 * LANES
    for c0 in range(0, MLA_HEADS * LANES, width):
        r = _dot(cq, wuq_ref[:, c0:c0 + width])
        partner = _dot(cq, wrot_ref[:, c0:c0 + width])
        for j in range(0, width, LANES):
            blk = slice(j, j + LANES)
            o_ref[:, c0 + j:c0 + j + LANES] = (r[:, blk] * cosv + partner[:, blk] * sinv).astype(BF16)


def _pos_tables(tables, m, t):
    tm = _row_tile(m)
    if tm <= t:
        assert t % tm == 0
        n = t // tm
        return tables, pl.BlockSpec((tm, LANES), lambda i: (i % n, 0))
    assert tm % t == 0
    return [jnp.tile(tb, (tm // t, 1)) for tb in tables], pl.BlockSpec((tm, LANES), lambda i: (0, 0))


def _rotary_partner_cols(w_pad):
    k = w_pad.shape[0]
    w3 = w_pad.reshape(k, -1, LANES)
    half = MLA_ROPE_DIM // 2
    lane = jnp.arange(LANES)
    first = (lane >= MLA_NOPE_DIM) & (lane < MLA_NOPE_DIM + half)
    second = (lane >= MLA_NOPE_DIM + half) & (lane < MLA_QK_DIM)
    rot = jnp.where(first, jnp.roll(w3, -half, axis=2), jnp.where(second, jnp.roll(w3, half, axis=2), 0))
    return rot.reshape(w_pad.shape)


def _mla_q(x, w_dq, g_q, w_uq_pad, w_uq_rot, tables, t):
    m = x.shape[0]
    tm = _row_tile(m)
    (cos_t, sin_t), tspec = _pos_tables(tables, m, t)
    row = lambda i: (i, 0)
    return pl.pallas_call(
        _mla_q_body,
        grid=(m // tm,),
        in_specs=[pl.BlockSpec((tm, D_MODEL), row), _resident(w_dq.shape), _resident((1, MLA_Q_RANK)),
                  _resident(w_uq_pad.shape), _resident(w_uq_rot.shape), tspec, tspec],
        out_specs=pl.BlockSpec((tm, MLA_HEADS * LANES), row),
        out_shape=jax.ShapeDtypeStruct((m, MLA_HEADS * LANES), BF16),
        compiler_params=_cparams(1),
        name="mla_q",
    )(x, w_dq, g_q.reshape(1, MLA_Q_RANK), w_uq_pad, w_uq_rot, cos_t, sin_t)


def _mla_kv_down_body(x_ref, w_ref, g_ref, cos_ref, sin_ref, ckv_ref, kr_ref):
    kv = _dot(x_ref[...].astype(BF16), w_ref[...])
    ckv_ref[...] = _rms(kv[:, :MLA_KV_RANK], g_ref[...])
    kr_ref[...] = _rope_block(kv[:, MLA_KV_RANK:], cos_ref[...], sin_ref[...])


def _mla_kv_down(x, w_dkv_pad, g_kv, tables, t):
    m = x.shape[0]
    tm = _row_tile(m)
    (cos_t, sin_t), tspec = _pos_tables(tables, m, t)
    row = lambda i: (i, 0)
    return pl.pallas_call(
        _mla_kv_down_body,
        grid=(m // tm,),
        in_specs=[pl.BlockSpec((tm, D_MODEL), row), _resident(w_dkv_pad.shape),
                  _resident((1, MLA_KV_RANK)), tspec, tspec],
        out_specs=[pl.BlockSpec((tm, MLA_KV_RANK), row), pl.BlockSpec((tm, LANES), row)],
        out_shape=[jax.ShapeDtypeStruct((m, MLA_KV_RANK), F32), jax.ShapeDtypeStruct((m, LANES), F32)],
        compiler_params=_cparams(1),
        name="mla_kv_down",
    )(x, w_dkv_pad, g_kv.reshape(1, MLA_KV_RANK), cos_t, sin_t)


def _mla_kv_up_body(c_ref, kr_ref, wuk_ref, wuv_ref, k_ref, v_ref):
    cb = c_ref[...].astype(BF16)
    kr = kr_ref[...]
    width = 4 * LANES
    for c0 in range(0, MLA_HEADS * LANES, width):
        r = _dot(cb, wuk_ref[:, c0:c0 + width])
        for j in range(0, width, LANES):
            k_ref[:, c0 + j:c0 + j + LANES] = (r[:, j:j + LANES] + kr).astype(BF16)
    for c0 in range(0, MLA_HEADS * MLA_V_DIM, width):
        v_ref[:, c0:c0 + width] = _dot(cb, wuv_ref[:, c0:c0 + width]).astype(BF16)


def _mla_kv_up(ckv, kr_pad, w_uk_pad, w_uv):
    m = ckv.shape[0]
    tm = next(c for c in (ROW_TILE, 3 * LANES, 2 * LANES, LANES, LANES // 8) if c <= ROW_TILE and m % c == 0)
    row = lambda i: (i, 0)
    return pl.pallas_call(
        _mla_kv_up_body,
        grid=(m // tm,),
        in_specs=[pl.BlockSpec((tm, MLA_KV_RANK), row), pl.BlockSpec((tm, LANES), row),
                  _resident(w_uk_pad.shape), _resident(w_uv.shape)],
        out_specs=[pl.BlockSpec((tm, MLA_HEADS * LANES), row), pl.BlockSpec((tm, MLA_HEADS * MLA_V_DIM), row)],
        out_shape=[jax.ShapeDtypeStruct((m, MLA_HEADS * LANES), BF16),
                   jax.ShapeDtypeStruct((m, MLA_HEADS * MLA_V_DIM), BF16)],
        compiler_params=_cparams(1),
        name="mla_kv_up",
    )(ckv, kr_pad, w_uk_pad, w_uv)


def _pool_body(prev_ref, halo_ref, x_ref, wp_ref, sc_ref, g_ref, b_ref, o_ref, ext, *, tm, pos0):
    i = pl.program_id(1)
    x = x_ref[0]
    ext[HALO:, :] = x

    @pl.when(i == 0)
    def _():
        ext[:HALO, :] = prev_ref[0]

    @pl.when(i > 0)
    def _():
        ext[:HALO, :] = halo_ref[0]

    pos = pos0 + i * tm + lax.broadcasted_iota(I32, (tm, 1), 0)
    for gi, w in enumerate(POOL_WINDOWS):
        ch = slice(gi * POOL_GROUP, (gi + 1) * POOL_GROUP)
        win = x[:, ch]
        for j in range(1, w):
            win = win + ext[HALO - j:HALO - j + tm, ch]
        count = jnp.minimum(pos + 1, w).astype(F32)
        mixed = (win / count - x[:, ch]).astype(BF16)
        o_ref[0, :, ch] = _dot(mixed, wp_ref[gi])
    y = o_ref[0] * sc_ref[...]
    o_ref[0] = _layer_norm(ALPHA * x + y, g_ref[...], b_ref[...])


def _pool_ln(x, prev_rows, pos0, w_pool, scale, g, b):
    bsz, t, _ = x.shape
    tm = _row_tile(t)
    prev = jnp.concatenate([jnp.zeros((bsz, HALO - POOL_STATE, D_MODEL), F32), prev_rows], axis=1)
    per = tm // HALO
    tile = lambda b_, i: (b_, i, 0)
    vec = pl.BlockSpec((1, D_MODEL), lambda b_, i: (0, 0))
    return pl.pallas_call(
        functools.partial(_pool_body, tm=tm, pos0=pos0),
        grid=(bsz, t // tm),
        in_specs=[pl.BlockSpec((1, HALO, D_MODEL), lambda b_, i: (b_, 0, 0)),
                  pl.BlockSpec((1, HALO, D_MODEL), lambda b_, i: (b_, jnp.maximum(i * per - 1, 0), 0)),
                  pl.BlockSpec((1, tm, D_MODEL), tile),
                  pl.BlockSpec(w_pool.shape, lambda b_, i: (0, 0, 0)), vec, vec, vec],
        out_specs=pl.BlockSpec((1, tm, D_MODEL), tile),
        out_shape=jax.ShapeDtypeStruct((bsz, t, D_MODEL), F32),
        scratch_shapes=[pltpu.VMEM((tm + HALO, D_MODEL), F32)],
        compiler_params=_cparams(2),
        name="pool_ln",
    )(prev, x, x, w_pool, scale.reshape(1, D_MODEL), g.reshape(1, D_MODEL), b.reshape(1, D_MODEL))


def _gla_proj_body(x_ref, win_ref, wa1_ref, wa2_ref, ba_ref, q_ref, k_ref, v_ref, gt_ref, la_ref):
    xb = x_ref[...].astype(BF16)
    width = 2 * MXU_DIM
    for o_ref, c0, c1 in ((q_ref, 0, GLA_KEY), (k_ref, GLA_KEY, 2 * GLA_KEY),
                          (v_ref, 2 * GLA_KEY, 2 * GLA_KEY + GLA_VAL),
                          (gt_ref, 2 * GLA_KEY + GLA_VAL, 2 * GLA_KEY + 2 * GLA_VAL)):
        for j0 in range(c0, c1, width):
            o_ref[:, j0 - c0:j0 - c0 + width] = _dot(xb, win_ref[:, j0:j0 + width]).astype(o_ref.dtype)
    low = _dot(xb, wa1_ref[...]).astype(BF16)
    z = _dot(low, wa2_ref[...]) + ba_ref[...]
    la_ref[...] = (jnp.minimum(z, 0.0) - jnp.log(1.0 + jnp.exp(-jnp.abs(z)))) / GLA_TAU


def _gla_proj(x, w_in, w_a1_pad, w_a2_pad, b_a):
    m = x.shape[0]
    tm = _row_tile(m)
    row = lambda i: (i, 0)
    widths = (GLA_KEY, GLA_KEY, GLA_VAL, GLA_VAL, GLA_KEY)
    dtypes = (F32, F32, BF16, F32, F32)
    return pl.pallas_call(
        _gla_proj_body,
        grid=(m // tm,),
        in_specs=[pl.BlockSpec((tm, D_MODEL), row), _resident(w_in.shape), _resident(w_a1_pad.shape),
                  _resident(w_a2_pad.shape), _resident((1, GLA_KEY))],
        out_specs=[pl.BlockSpec((tm, w), row) for w in widths],
        out_shape=[jax.ShapeDtypeStruct((m, w), d) for w, d in zip(widths, dtypes)],
        compiler_params=_cparams(1),
        name="gla_proj",
    )(x, w_in, w_a1_pad, w_a2_pad, b_a.reshape(1, GLA_KEY))


def _cumsum_rows(z):
    n = z.shape[0]
    row = lax.broadcasted_iota(I32, z.shape, 0)
    shift = 1
    while shift < n:
        z = z + jnp.where(row >= shift, pltpu.roll(z, shift, 0), 0.0)
        shift *= 2
    return z


def _gla_body(q_ref, k_ref, la_ref, v_ref, gt_ref, s0_ref, hg_ref, o_ref, s_ref, state, *,
              blk, n_blk, unroll):
    c = pl.program_id(1)

    @pl.when(c == 0)
    def _():
        state[...] = s0_ref[0]

    row = lax.broadcasted_iota(I32, (blk, blk), 0)
    col = lax.broadcasted_iota(I32, (blk, blk), 1)
    causal = row >= col

    def block(r0, h):
        rows = pl.ds(r0, blk)
        kc = slice(h * GLA_DK, (h + 1) * GLA_DK)
        vc = slice(h * GLA_DV, (h + 1) * GLA_DV)
        cum = _cumsum_rows(la_ref[0, rows, kc])
        cum_last = cum[blk - 1:blk, :]
        q = q_ref[0, rows, kc] * (GLA_DK ** -0.5)
        k = k_ref[0, rows, kc]
        v = v_ref[0, rows, vc]
        q_dec = (q * jnp.exp(cum)).astype(BF16)
        k_inv = (k * jnp.exp(-cum)).astype(BF16)
        k_end = (k * jnp.exp(cum_last - cum)).astype(BF16)
        scores = jnp.where(causal, _dot_nt(q_dec, k_inv), 0.0).astype(BF16)
        s_old = state[h]
        o = _dot(scores, v) + _dot_nt(q_dec, s_old.astype(BF16))
        state[h] = jnp.exp(cum_last) * s_old + _dot_tn(v, k_end)
        gate = gt_ref[0, rows, vc]
        o_ref[0, rows, vc] = (_rms(o, hg_ref[...]) * jax.nn.silu(gate)).astype(BF16)

    def step(j, carry):
        for u in range(unroll):
            r0 = pl.multiple_of((j * unroll + u) * blk, blk)
            for h in range(GLA_HEADS):
                block(r0, h)
        return carry

    lax.fori_loop(0, n_blk // unroll, step, 0)

    @pl.when(c == pl.num_programs(1) - 1)
    def _():
        s_ref[0] = state[...]


def _gla(q, k, la, v, gate, s0_t, head_g):
    bsz, t, _ = q.shape
    blk = GLA_BLOCK if t % GLA_BLOCK == 0 else t
    tc = _row_tile(t)
    n_blk = tc // blk
    kspec = pl.BlockSpec((1, tc, GLA_KEY), lambda b_, c: (b_, c, 0))
    vspec = pl.BlockSpec((1, tc, GLA_VAL), lambda b_, c: (b_, c, 0))
    sspec = pl.BlockSpec((1, GLA_HEADS, GLA_DV, GLA_DK), lambda b_, c: (b_, 0, 0, 0))
    return pl.pallas_call(
        functools.partial(_gla_body, blk=blk, n_blk=n_blk, unroll=2 if n_blk % 2 == 0 else 1),
        grid=(bsz, t // tc),
        in_specs=[kspec, kspec, kspec, vspec, vspec, sspec, pl.BlockSpec((1, GLA_DV), lambda b_, c: (0, 0))],
        out_specs=[vspec, sspec],
        out_shape=[jax.ShapeDtypeStruct((bsz, t, GLA_VAL), BF16),
                   jax.ShapeDtypeStruct((bsz, GLA_HEADS, GLA_DV, GLA_DK), F32)],
        scratch_shapes=[pltpu.VMEM((GLA_HEADS, GLA_DV, GLA_DK), F32)],
        compiler_params=_cparams(2),
        name="gla",
    )(q, k, la, v, gate, s0_t, head_g.reshape(1, GLA_DV))


def _pad_head_cols(w, heads, width):
    k = w.shape[0]
    w = w.reshape(k, heads, width)
    return jnp.pad(w, ((0, 0), (0, 0), (0, LANES - width))).reshape(k, heads * LANES)


def kernel(x_prompt, x_sample, cache_a_k, cache_a_v, cache_b_ckv, cache_b_krope, state_c_rows, state_d_gla, a_w_qkv, a_lam_q1, a_lam_k1, a_lam_q2, a_lam_k2, a_subln_g, a_w_o, b_w_dq, b_q_norm_g, b_w_uq, b_w_dkv, b_kv_norm_g, b_w_uk, b_w_uv, b_w_o, c_w_pool, c_scale, d_w_in, d_w_a1, d_w_a2, d_b_a, d_head_g, d_w_o, ln_g, ln_b, ffn_w_in, ffn_w_out, moe_w_router, moe_b_router, moe_w_in, moe_w_out):
    bf = lambda w: w.astype(BF16)
    d = D_MODEL
    past = cache_a_k.shape[1]
    streams = [(x_prompt.shape[0], x_prompt.shape[1], 0), (x_sample.shape[0], x_sample.shape[1], past)]
    xs = [x_prompt.reshape(-1, d), x_sample.reshape(-1, d)]

    ffn_in, ffn_out = bf(ffn_w_in), bf(ffn_w_out)
    moe_in, moe_out = bf(moe_w_in), bf(moe_w_out)

    w_qkv, w_o = bf(a_w_qkv), bf(a_w_o)
    lam_vecs = jnp.stack([a_lam_q1, a_lam_k1, a_lam_q2, a_lam_k2]).astype(F32)
    lambda_init = 0.8 - 0.6 * math.exp(-0.3 * 0)
    w_k_t = bf(a_w_qkv[:, d:2 * d].T)
    caches = [(None, None), (cache_a_k, cache_a_v)]
    a_k, a_v = [], []
    for s, ((bsz, t, p), (ck, cv)) in enumerate(zip(streams, caches)):
        q_sink = (0, d, [(BF16, DA_HEAD_DIM ** -0.5)])
        v_sinks = (2 * d, 3 * d, [(F32, 1.0), (BF16, 1.0)])
        if t % _row_tile(bsz * t) == 0:
            qb, kb, vf, vb, k_t = _proj(
                xs[s], w_qkv, [q_sink, (d, 2 * d, [(BF16, 1.0)]), v_sinks], name="a_qkv",
                transposed=(w_k_t, bsz, t))
            k_out = jnp.transpose(k_t.reshape(bsz, DA_HEADS, 2, DA_HEAD_DIM, t), (0, 4, 1, 2, 3))
        else:
            qb, kf, kb, vf, vb = _proj(
                xs[s], w_qkv, [q_sink, (d, 2 * d, [(F32, 1.0), (BF16, 1.0)]), v_sinks], name="a_qkv")
            k_out = kf.reshape(bsz, t, DA_HEADS, 2, DA_HEAD_DIM)
        kb, vb = kb.reshape(bsz, t, d), vb.reshape(bsz, t, d)
        if p:
            kb = jnp.concatenate([bf(ck.reshape(bsz, p, d)), kb], axis=1)
            vb = jnp.concatenate([bf(cv.reshape(bsz, p, d)), vb], axis=1)
        o = _attention("diff", qb.reshape(bsz, t, d), kb, vb, lam_vecs=lam_vecs, subln_g=a_subln_g,
                       lambda_init=lambda_init, name="a_attn")
        xs[s] = _oproj_ln(o.reshape(-1, d), w_o, xs[s], ln_g[0, 0], ln_b[0, 0], name="a_out_ln")
        xs[s] = _dense_ffn(xs[s], ffn_in, ffn_out, 0, ln_g[0, 1], ln_b[0, 1], name="ffn0")
        a_k.append(k_out)
        a_v.append(vf.reshape(bsz, t, DA_HEADS, 2 * DA_HEAD_DIM))

    w_dq, w_o = bf(b_w_dq), bf(b_w_o)
    w_uq_pad = bf(_pad_head_cols(b_w_uq, MLA_HEADS, MLA_QK_DIM))
    w_uq_rot = _rotary_partner_cols(w_uq_pad)
    w_uk_pad = bf(_pad_head_cols(b_w_uk, MLA_HEADS, MLA_NOPE_DIM))
    w_uv = bf(b_w_uv)
    w_dkv_pad = bf(jnp.concatenate(
        [b_w_dkv[:, :MLA_KV_RANK], jnp.zeros((d, MLA_NOPE_DIM), F32), b_w_dkv[:, MLA_KV_RANK:],
         jnp.zeros((d, LANES - MLA_QK_DIM), F32)], axis=1))
    caches = [(None, None), (cache_b_ckv, cache_b_krope)]
    b_ckv, b_kr = [], []
    for s, ((bsz, t, p), (cc, ckr)) in enumerate(zip(streams, caches)):
        tables = _rope_tables(t, p)
        q = _mla_q(xs[s], w_dq, b_q_norm_g, w_uq_pad, w_uq_rot, tables, t)
        ckv, kr_pad = _mla_kv_down(xs[s], w_dkv_pad, b_kv_norm_g, tables, t)
        ckv_all, kr_all = ckv, kr_pad
        if p:
            ckr_pad = jnp.pad(ckr, ((0, 0), (0, 0), (MLA_NOPE_DIM, LANES - MLA_QK_DIM)))
            ckv_all = jnp.concatenate([cc, ckv.reshape(bsz, t, -1)], axis=1).reshape(-1, MLA_KV_RANK)
            kr_all = jnp.concatenate([ckr_pad, kr_pad.reshape(bsz, t, -1)], axis=1).reshape(-1, LANES)
        k_pad, v = _mla_kv_up(ckv_all, kr_all, w_uk_pad, w_uv)
        o = _attention("pair", q.reshape(bsz, t, -1), k_pad.reshape(bsz, p + t, -1),
                       v.reshape(bsz, p + t, -1), scale=MLA_QK_DIM ** -0.5, name="b_attn")
        xs[s] = _oproj_ln(o.reshape(-1, d), w_o, xs[s], ln_g[1, 0], ln_b[1, 0], name="b_out_ln")
        xs[s] = _moe(xs[s], moe_w_router[0], moe_b_router[0], moe_in[0], moe_out[0], ln_g[1, 1], ln_b[1, 1])
        b_ckv.append(ckv.reshape(bsz, t, MLA_KV_RANK))
        b_kr.append(kr_pad[:, MLA_NOPE_DIM:MLA_QK_DIM].reshape(bsz, t, MLA_ROPE_DIM))

    w_pool = bf(c_w_pool)
    prevs = [jnp.zeros((streams[0][0], POOL_STATE, d), F32), state_c_rows]
    c_rows = []
    for s, ((bsz, t, p), prev) in enumerate(zip(streams, prevs)):
        x3 = xs[s].reshape(bsz, t, d)
        c_rows.append(jnp.concatenate([prev, x3], axis=1)[:, -POOL_STATE:])
        xs[s] = _pool_ln(x3, prev, p, w_pool, c_scale, ln_g[2, 0], ln_b[2, 0]).reshape(-1, d)
        xs[s] = _dense_ffn(xs[s], ffn_in, ffn_out, 1, ln_g[2, 1], ln_b[2, 1], name="ffn1")

    w_in, w_o = bf(d_w_in), bf(d_w_o)
    w_a1_pad = bf(jnp.pad(d_w_a1, ((0, 0), (0, LANES - GLA_GATE_RANK))))
    w_a2_pad = bf(jnp.pad(d_w_a2, ((0, LANES - GLA_GATE_RANK), (0, 0))))
    states = [jnp.zeros((streams[0][0], GLA_HEADS, GLA_DK, GLA_DV), F32), state_d_gla]
    d_gla = []
    for s, ((bsz, t, p), s0) in enumerate(zip(streams, states)):
        q, k, v, gate, la = _gla_proj(xs[s], w_in, w_a1_pad, w_a2_pad, d_b_a)
        shp = lambda z: z.reshape(bsz, t, -1)
        o, s_t = _gla(shp(q), shp(k), shp(la), shp(v), shp(gate), jnp.swapaxes(s0, 2, 3), d_head_g)
        xs[s] = _oproj_ln(o.reshape(-1, d), w_o, xs[s], ln_g[3, 0], ln_b[3, 0], name="d_out_ln")
        xs[s] = _moe(xs[s], moe_w_router[1], moe_b_router[1], moe_in[1], moe_out[1], ln_g[3, 1], ln_b[3, 1])
        d_gla.append(jnp.swapaxes(s_t, 2, 3))

    y = [xs[s].reshape(bsz, t, d) for s, (bsz, t, _) in enumerate(streams)]
    return (y[0], y[1], a_k[0], a_v[0], a_k[1], a_v[1], b_ckv[0], b_kr[0], b_ckv[1], b_kr[1],
            c_rows[0], c_rows[1], d_gla[0], d_gla[1])
```
